```python
import jax, jax.numpy as jnp
from jax import lax
import numpy as np

D_MODEL = 1024
BATCH = 8
SEQ = 8192
DEPTH = 1
DEC_BATCH = 8
DEC_SEQ = 64
PAST_LEN = 1024

CHUNK = 64
Q_BLOCK = 128
PLE_DIM = 256
N_HEADS = 8
QK_NOPE = 64
QK_ROPE = 32
V_DIM = 64
Q_LORA = 384
KV_LORA = 256
ROPE_BASE = 10000.0
CONV_CH = 512
CONV_K = 31
ATTN_W = N_HEADS * V_DIM
MIX_W = ATTN_W + CONV_CH
D_FF = 4 * D_MODEL
IN_W = Q_LORA + KV_LORA + QK_ROPE + 2 * CONV_CH
QK_DIM = QK_NOPE + QK_ROPE
EPS = 1e-6
LN_EPS = 1e-5
NEG = -1e30

kernel_name = 'hybrid_mla_conformer_stream_step'


def rms_norm(x, g):
    xf = x.astype(jnp.float32)
    y = xf * lax.rsqrt(jnp.mean(xf * xf, axis=-1, keepdims=True) + EPS)
    return (y * g.astype(jnp.float32)).astype(x.dtype)


def layer_norm(x, g, b):
    xf = x.astype(jnp.float32)
    mu = jnp.mean(xf, axis=-1, keepdims=True)
    var = jnp.mean(jnp.square(xf - mu), axis=-1, keepdims=True)
    y = (xf - mu) * lax.rsqrt(var + LN_EPS)
    return (y * g.astype(jnp.float32) + b.astype(jnp.float32)).astype(x.dtype)


def rope(x, pos):
    half = QK_ROPE // 2
    inv = ROPE_BASE ** (-jnp.arange(half, dtype=jnp.float32) / half)
    ang = pos.astype(jnp.float32)[:, None] * inv[None, :]
    ang = ang.reshape(ang.shape[:1] + (1,) * (x.ndim - 3) + ang.shape[1:])
    cos, sin = jnp.cos(ang), jnp.sin(ang)
    xf = x.astype(jnp.float32)
    x1, x2 = xf[..., :half], xf[..., half:]
    return jnp.concatenate([x1 * cos - x2 * sin, x1 * sin + x2 * cos], axis=-1).astype(x.dtype)


def expand_kv(c_kv, k_r, w_ukv):
    kv = jnp.einsum('btc,chd->bthd', c_kv, w_ukv.reshape(KV_LORA, N_HEADS, QK_NOPE + V_DIM))
    k_nope, v = kv[..., :QK_NOPE], kv[..., QK_NOPE:]
    k_rope = jnp.broadcast_to(k_r[:, :, None, :], k_nope.shape[:3] + (QK_ROPE,))
    return jnp.concatenate([k_nope, k_rope], axis=-1), v


def attend_chunk_causal(q, k, v):
    b, s = q.shape[:2]
    n_blk = s // Q_BLOCK
    scale = QK_DIM ** -0.5
    q_blocks = q.reshape(b, n_blk, Q_BLOCK, N_HEADS, QK_DIM).swapaxes(0, 1)
    k_chunk = jnp.arange(s) // CHUNK

    def one_block(args):
        q_blk, blk = args
        q_chunk = (blk * Q_BLOCK + jnp.arange(Q_BLOCK)) // CHUNK
        mask = k_chunk[None, :] <= q_chunk[:, None]
        sc = jnp.einsum('bqhd,bkhd->bhqk', q_blk, k).astype(jnp.float32) * scale
        pr = jax.nn.softmax(jnp.where(mask, sc, NEG), axis=-1).astype(v.dtype)
        return jnp.einsum('bhqk,bkhd->bqhd', pr, v)

    out = lax.map(one_block, (q_blocks, jnp.arange(n_blk)))
    return out.swapaxes(0, 1).reshape(b, s, ATTN_W)


def attend_open(q, k, v):
    b, s = q.shape[:2]
    sc = jnp.einsum('bqhd,bkhd->bhqk', q, k).astype(jnp.float32) * (QK_DIM ** -0.5)
    pr = jax.nn.softmax(sc, axis=-1).astype(v.dtype)
    return jnp.einsum('bhqk,bkhd->bqhd', pr, v).reshape(b, s, ATTN_W)


def causal_depthwise_conv(u, past, w, bias):
    padded = jnp.concatenate([past, u], axis=1)
    y = lax.conv_general_dilated(padded, w[:, None, :], window_strides=(1,), padding='VALID',
                                 dimension_numbers=('NWC', 'WIO', 'NWC'),
                                 feature_group_count=CONV_CH)
    return y + bias, padded[:, -(CONV_K - 1):]


def layer(x, p, pos, past, lp):
    b, s, _ = x.shape
    h = rms_norm(x, lp['norm_mix_g'])
    z = h @ lp['w_in']
    c_q, c_kv, k_r, g_in = jnp.split(z, [Q_LORA, Q_LORA + KV_LORA, Q_LORA + KV_LORA + QK_ROPE], axis=-1)
    q = (rms_norm(c_q, lp['q_norm_g']) @ lp['w_uq']).reshape(b, s, N_HEADS, QK_DIM)
    q = jnp.concatenate([q[..., :QK_NOPE], rope(q[..., QK_NOPE:], pos)], axis=-1)
    c_kv = rms_norm(c_kv, lp['kv_norm_g'])
    k_r = rope(k_r, pos)
    u = g_in[..., :CONV_CH] * jax.nn.sigmoid(g_in[..., CONV_CH:])
    if past is None:
        ckv_all, kr_all = c_kv, k_r
        conv_past = jnp.zeros((b, CONV_K - 1, CONV_CH), u.dtype)
    else:
        ckv_all = jnp.concatenate([past[0], c_kv], axis=1)
        kr_all = jnp.concatenate([past[1], k_r], axis=1)
        conv_past = past[2]
    k, v = expand_kv(ckv_all, kr_all, lp['w_ukv'])
    attn = attend_chunk_causal(q, k, v) if past is None else attend_open(q, k, v)
    conv, conv_new = causal_depthwise_conv(u, conv_past, lp['conv_w'], lp['conv_b'])
    conv = jax.nn.silu(layer_norm(conv, lp['conv_ln_g'], lp['conv_ln_b']))
    mixed = jnp.concatenate([rms_norm(attn, lp['attn_out_g']), rms_norm(conv, lp['conv_out_g'])], axis=-1)
    x = x + mixed @ lp['w_out']
    f = rms_norm(x, lp['norm_ffn_g']) @ lp['w_ff_up']
    x = x + jnp.square(jax.nn.relu(f)) @ lp['w_ff_down']
    x = x + jax.nn.sigmoid(rms_norm(x, lp['norm_ple_g']) @ lp['w_ple_gate']) * (p @ lp['w_ple_proj'])
    return x, c_kv, k_r, conv_new


def setup_inputs(seed: int = 0) -> dict:
    key = jax.random.key(seed)
    ks = iter(jax.random.split(key, 40))
    f32 = jnp.float32

    def nrm(shape, scale):
        return jax.random.normal(next(ks), shape, f32) * scale

    def gain(dim):
        return 1.0 + nrm((DEPTH, dim), 0.02)

    return {
        'x_prompt': nrm((BATCH, SEQ, D_MODEL), 1.0),
        'x_sample': nrm((DEC_BATCH, DEC_SEQ, D_MODEL), 1.0),
        'cache_ckv': nrm((DEPTH, DEC_BATCH, PAST_LEN, KV_LORA), 1.0),
        'cache_krope': nrm((DEPTH, DEC_BATCH, PAST_LEN, QK_ROPE), 1.0),
        'state_conv': nrm((DEPTH, DEC_BATCH, CONV_K - 1, CONV_CH), 0.5),
        'p_prompt': nrm((DEPTH, BATCH, SEQ, PLE_DIM), 1.0),
        'p_sample': nrm((DEPTH, DEC_BATCH, DEC_SEQ, PLE_DIM), 1.0),
        'norm_mix_g': gain(D_MODEL),
        'w_in': nrm((DEPTH, D_MODEL, IN_W), D_MODEL ** -0.5),
        'q_norm_g': gain(Q_LORA),
        'w_uq': nrm((DEPTH, Q_LORA, N_HEADS * QK_DIM), Q_LORA ** -0.5),
        'kv_norm_g': gain(KV_LORA),
        'w_ukv': nrm((DEPTH, KV_LORA, N_HEADS * (QK_NOPE + V_DIM)), KV_LORA ** -0.5),
        'conv_w': nrm((DEPTH, CONV_K, CONV_CH), CONV_K ** -0.5),
        'conv_b': nrm((DEPTH, CONV_CH), 0.02),
        'conv_ln_g': gain(CONV_CH),
        'conv_ln_b': nrm((DEPTH, CONV_CH), 0.02),
        'attn_out_g': gain(ATTN_W),
        'conv_out_g': gain(CONV_CH),
        'w_out': nrm((DEPTH, MIX_W, D_MODEL), MIX_W ** -0.5),
        'norm_ffn_g': gain(D_MODEL),
        'w_ff_up': nrm((DEPTH, D_MODEL, D_FF), D_MODEL ** -0.5),
        'w_ff_down': nrm((DEPTH, D_FF, D_MODEL), D_FF ** -0.5),
        'norm_ple_g': gain(D_MODEL),
        'w_ple_gate': nrm((DEPTH, D_MODEL, D_MODEL), D_MODEL ** -0.5),
        'w_ple_proj': nrm((DEPTH, PLE_DIM, D_MODEL), PLE_DIM ** -0.5),
        'norm_final_g': 1.0 + nrm((D_MODEL,), 0.02),
    }


def reference(x_prompt, x_sample, cache_ckv, cache_krope, state_conv, p_prompt, p_sample,
              norm_mix_g, w_in, q_norm_g, w_uq, kv_norm_g, w_ukv, conv_w, conv_b, conv_ln_g, conv_ln_b,
              attn_out_g, conv_out_g, w_out, norm_ffn_g, w_ff_up, w_ff_down, norm_ple_g, w_ple_gate,
              w_ple_proj, norm_final_g):
    s_prompt = x_prompt.shape[1]
    s_sample = x_sample.shape[1]
    pos_prompt = jnp.arange(s_prompt)
    pos_sample = PAST_LEN + jnp.arange(s_sample)
    hp, hs = x_prompt, x_sample
    ckv_p, kr_p, conv_p, ckv_s, kr_s, conv_s = [], [], [], [], [], []
    for i in range(DEPTH):
        lp = {
            'norm_mix_g': norm_mix_g[i], 'w_in': w_in[i], 'q_norm_g': q_norm_g[i], 'w_uq': w_uq[i],
            'kv_norm_g': kv_norm_g[i], 'w_ukv': w_ukv[i], 'conv_w': conv_w[i], 'conv_b': conv_b[i],
            'conv_ln_g': conv_ln_g[i], 'conv_ln_b': conv_ln_b[i], 'attn_out_g': attn_out_g[i],
            'conv_out_g': conv_out_g[i], 'w_out': w_out[i], 'norm_ffn_g': norm_ffn_g[i],
            'w_ff_up': w_ff_up[i], 'w_ff_down': w_ff_down[i], 'norm_ple_g': norm_ple_g[i],
            'w_ple_gate': w_ple_gate[i], 'w_ple_proj': w_ple_proj[i],
        }
        hp, a, b, c = layer(hp, p_prompt[i], pos_prompt, None, lp)
        ckv_p.append(a); kr_p.append(b); conv_p.append(c)
        hs, a, b, c = layer(hs, p_sample[i], pos_sample, (cache_ckv[i], cache_krope[i], state_conv[i]), lp)
        ckv_s.append(a); kr_s.append(b); conv_s.append(c)
    y_prompt = rms_norm(hp, norm_final_g)
    y_sample = rms_norm(hs, norm_final_g)
    new_ckv_prompt = jnp.stack(ckv_p)
    new_krope_prompt = jnp.stack(kr_p)
    new_conv_prompt = jnp.stack(conv_p)
    new_ckv_sample = jnp.stack(ckv_s)
    new_krope_sample = jnp.stack(kr_s)
    new_conv_sample = jnp.stack(conv_s)
    return (y_prompt, y_sample, new_ckv_prompt, new_krope_prompt, new_conv_prompt,
            new_ckv_sample, new_krope_sample, new_conv_sample)
```

```python
import functools

import jax
import jax.numpy as jnp
from jax import lax
from jax.experimental import pallas as pl
from jax.experimental.pallas import tpu as pltpu

F32 = jnp.float32
BF16 = jnp.bfloat16

CHUNK = 64
N_HEADS = 8
QK_NOPE = 64
QK_ROPE = 32
V_DIM = 64
Q_LORA = 384
KV_LORA = 256
ROPE_BASE = 10000.0
CONV_CH = 512
CONV_K = 31
QK_DIM = QK_NOPE + QK_ROPE
EPS = 1e-6
LN_EPS = 1e-5
NEG = -1e30

LANE = 128
HEAD_W = N_HEADS * LANE
HALO = 32
CONV_ROWS = 32
VMEM_LIMIT = 56 * 1024 * 1024


def _rms(x, g):
    return x * lax.rsqrt(jnp.mean(x * x, axis=-1, keepdims=True) + EPS) * g


def _dot(a, b):
    return jnp.dot(a, b, preferred_element_type=F32)


def _full(shape):
    return pl.BlockSpec(shape, lambda *_: (0,) * len(shape))


def _resident(shape):
    return pl.BlockSpec(shape, lambda *_: (0,) * len(shape), pipeline_mode=pl.Buffered(1))


def _inproj_kernel(x_ref, tqc_ref, tqs_ref, tk_ref, gmix_ref, win_ref, gq_ref, wuq_ref, gkv_ref, wkv_ref,
                   q_ref, k_ref, v_ref, ckv_ref, kr_ref, u_ref):
    h = _rms(x_ref[...], gmix_ref[...]).astype(BF16)
    z = _dot(h, win_ref[...])
    o_kv = Q_LORA
    o_ga = o_kv + KV_LORA
    o_gb = o_ga + CONV_CH
    o_kr = o_gb + CONV_CH

    cq = _rms(z[:, :o_kv], gq_ref[...]).astype(BF16)
    qq = _dot(cq, wuq_ref[...])
    tqc = tqc_ref[...]
    tqs = tqs_ref[...]
    for hd in range(N_HEADS):
        a = qq[:, hd * LANE:(hd + 1) * LANE]
        b = qq[:, HEAD_W + hd * LANE:HEAD_W + (hd + 1) * LANE]
        q_ref[:, hd * LANE:(hd + 1) * LANE] = (a * tqc + b * tqs).astype(BF16)

    ckv = _rms(z[:, o_kv:o_ga], gkv_ref[...])
    ckv_ref[...] = ckv
    kv = _dot(ckv.astype(BF16), wkv_ref[...])

    t = z[:, o_kr:o_kr + LANE] * tk_ref[...]
    krot = t + pltpu.roll(t, QK_ROPE, 1)
    kr_ref[...] = krot[:, :QK_ROPE]
    lane = lax.broadcasted_iota(jnp.int32, krot.shape, 1)
    kr_at_rope = jnp.where((lane >= QK_NOPE) & (lane < QK_DIM), krot, 0.0)
    one_at_sum = jnp.where(lane == V_DIM, 1.0, 0.0)
    for hd in range(N_HEADS):
        k_ref[:, hd * LANE:(hd + 1) * LANE] = (kv[:, hd * LANE:(hd + 1) * LANE] + kr_at_rope).astype(BF16)
        v_ref[:, hd * LANE:(hd + 1) * LANE] = (
            kv[:, HEAD_W + hd * LANE:HEAD_W + (hd + 1) * LANE] + one_at_sum).astype(BF16)

    u_ref[...] = z[:, o_ga:o_gb] * jax.nn.sigmoid(z[:, o_gb:o_kr])


def _inproj(x, tqc, tqs, tk, gmix, win, gq, wuq, gkv, wkv, tm):
    n, d = x.shape
    nt = tqc.shape[0] // tm
    row = lambda w: pl.BlockSpec((tm, w), lambda i: (i, 0))
    tab = pl.BlockSpec((tm, LANE), lambda i: (i % nt, 0))
    return pl.pallas_call(
        _inproj_kernel,
        grid=(n // tm,),
        in_specs=[row(d), tab, tab, tab, _full(gmix.shape), _resident(win.shape), _full(gq.shape),
                  _resident(wuq.shape), _full(gkv.shape), _resident(wkv.shape)],
        out_specs=[row(HEAD_W), row(HEAD_W), row(HEAD_W), row(KV_LORA), row(QK_ROPE), row(CONV_CH)],
        out_shape=[jax.ShapeDtypeStruct((n, HEAD_W), BF16), jax.ShapeDtypeStruct((n, HEAD_W), BF16),
                   jax.ShapeDtypeStruct((n, HEAD_W), BF16), jax.ShapeDtypeStruct((n, KV_LORA), F32),
                   jax.ShapeDtypeStruct((n, QK_ROPE), F32), jax.ShapeDtypeStruct((n, CONV_CH), F32)],
        compiler_params=pltpu.CompilerParams(dimension_semantics=("arbitrary",), vmem_limit_bytes=VMEM_LIMIT),
        name="inproj",
    )(x, tqc, tqs, tk, gmix, win, gq, wuq, gkv, wkv)


def _kvexpand_kernel(ckv_ref, krp_ref, wkv_ref, k_ref, v_ref):
    kv = _dot(ckv_ref[...].astype(BF16), wkv_ref[...])
    krp = krp_ref[...]
    lane = lax.broadcasted_iota(jnp.int32, krp.shape, 1)
    one_at_sum = jnp.where(lane == V_DIM, 1.0, 0.0)
    for hd in range(N_HEADS):
        k_ref[:, hd * LANE:(hd + 1) * LANE] = (kv[:, hd * LANE:(hd + 1) * LANE] + krp).astype(BF16)
        v_ref[:, hd * LANE:(hd + 1) * LANE] = (
            kv[:, HEAD_W + hd * LANE:HEAD_W + (hd + 1) * LANE] + one_at_sum).astype(BF16)


def _kvexpand(ckv, krp, wkv, tm):
    n = ckv.shape[0]
    row = lambda w: pl.BlockSpec((tm, w), lambda i: (i, 0))
    return pl.pallas_call(
        _kvexpand_kernel,
        grid=(n // tm,),
        in_specs=[row(KV_LORA), row(LANE), _resident(wkv.shape)],
        out_specs=[row(HEAD_W), row(HEAD_W)],
        out_shape=[jax.ShapeDtypeStruct((n, HEAD_W), BF16), jax.ShapeDtypeStruct((n, HEAD_W), BF16)],
        compiler_params=pltpu.CompilerParams(dimension_semantics=("arbitrary",), vmem_limit_bytes=VMEM_LIMIT),
        name="kvexpand",
    )(ckv, krp, wkv)


def _conv_kernel(u_ref, uprev_ref, past_ref, w_ref, b_ref, lng_ref, lnb_ref, og_ref, o_ref, ext_ref, *, tc):
    first = pl.program_id(1) == 0
    ext_ref[0:HALO, :] = jnp.where(first, past_ref[0], uprev_ref[0])
    ext_ref[HALO:, :] = u_ref[0]
    w = w_ref[...]
    lead = HALO - (CONV_K - 1)
    for r in range(tc // CONV_ROWS):
        base = r * CONV_ROWS + lead
        acc = jnp.broadcast_to(b_ref[...], (CONV_ROWS, CONV_CH))
        for j in range(CONV_K):
            acc = acc + w[j:j + 1, :] * ext_ref[base + j:base + j + CONV_ROWS, :]
        mu = jnp.mean(acc, axis=-1, keepdims=True)
        dlt = acc - mu
        var = jnp.mean(dlt * dlt, axis=-1, keepdims=True)
        y = dlt * lax.rsqrt(var + LN_EPS) * lng_ref[...] + lnb_ref[...]
        y = y * jax.nn.sigmoid(y)
        o_ref[0, r * CONV_ROWS:(r + 1) * CONV_ROWS, :] = _rms(y, og_ref[...])


def _conv(u, past, w, b, lng, lnb, og, tc):
    bsz, s, c = u.shape
    per = tc // HALO
    return pl.pallas_call(
        functools.partial(_conv_kernel, tc=tc),
        grid=(bsz, s // tc),
        in_specs=[pl.BlockSpec((1, tc, c), lambda bi, i: (bi, i, 0)),
                  pl.BlockSpec((1, HALO, c), lambda bi, i: (bi, jnp.maximum(i * per - 1, 0), 0)),
                  pl.BlockSpec((1, HALO, c), lambda bi, i: (bi, 0, 0)),
                  _full(w.shape), _full(b.shape), _full(lng.shape), _full(lnb.shape), _full(og.shape)],
        out_specs=pl.BlockSpec((1, tc, c), lambda bi, i: (bi, i, 0)),
        out_shape=jax.ShapeDtypeStruct((bsz, s, c), F32),
        scratch_shapes=[pltpu.VMEM((tc + HALO, c), F32)],
        compiler_params=pltpu.CompilerParams(dimension_semantics=("arbitrary", "arbitrary"),
                                             vmem_limit_bytes=VMEM_LIMIT),
        name="convmod",
    )(u, u, past, w, b, lng, lnb, og)


def _attn_step(q, kblk, vblk, m, acc, mask=None):
    s = lax.dot_general(q, kblk, (((1,), (1,)), ((), ())), preferred_element_type=F32)
    if mask is not None:
        s = jnp.where(mask, s, NEG)
    m_new = jnp.maximum(m, jnp.max(s, axis=-1, keepdims=True))
    p = jnp.exp(s - m_new)
    acc = jnp.exp(m - m_new) * acc + _dot(p.astype(BF16), vblk)
    return m_new, acc


def _attn_finish(acc):
    return (acc / acc[:, V_DIM:V_DIM + 1])[:, :V_DIM]


def _attn_causal_kernel(q_ref, k_ref, v_ref, o_ref, *, tq, tk):
    qi = pl.program_id(2)
    row_chunk = lax.broadcasted_iota(jnp.int32, (tq, tq), 0) // CHUNK
    col_chunk = lax.broadcasted_iota(jnp.int32, (tq, tq), 1) // CHUNK
    mask = col_chunk <= row_chunk
    outs = []
    for j in range(2):
        ls = slice(j * LANE, (j + 1) * LANE)
        q = q_ref[0, :, ls]

        def body(kb, carry, ls=ls, q=q):
            start = pl.multiple_of(kb * tk, tk)
            return _attn_step(q, k_ref[0, pl.ds(start, tk), ls], v_ref[0, pl.ds(start, tk), ls], *carry)

        m0 = jnp.full((tq, 1), NEG, F32)
        acc0 = jnp.zeros((tq, LANE), F32)
        m, acc = lax.fori_loop(0, qi * (tq // tk), body, (m0, acc0))
        dstart = pl.multiple_of(qi * tq, tq)
        m, acc = _attn_step(q, k_ref[0, pl.ds(dstart, tq), ls], v_ref[0, pl.ds(dstart, tq), ls], m, acc, mask)
        outs.append(_attn_finish(acc))
    o_ref[0] = jnp.concatenate(outs, axis=1)


def _attn_causal(q, k, v, tq, tk):
    bsz, s, _ = q.shape
    kv_spec = pl.BlockSpec((1, s, 2 * LANE), lambda bi, hp, qi: (bi, 0, hp))
    return pl.pallas_call(
        functools.partial(_attn_causal_kernel, tq=tq, tk=tk),
        grid=(bsz, N_HEADS // 2, s // tq),
        in_specs=[pl.BlockSpec((1, tq, 2 * LANE), lambda bi, hp, qi: (bi, qi, hp)), kv_spec, kv_spec],
        out_specs=pl.BlockSpec((1, tq, 2 * V_DIM), lambda bi, hp, qi: (bi, qi, hp)),
        out_shape=jax.ShapeDtypeStruct((bsz, s, N_HEADS * V_DIM), F32),
        compiler_params=pltpu.CompilerParams(dimension_semantics=("arbitrary",) * 3, vmem_limit_bytes=VMEM_LIMIT),
        name="attn_causal",
    )(q, k, v)


def _attn_open_kernel(q_ref, kc_ref, vc_ref, kn_ref, vn_ref, o_ref):
    tq = q_ref.shape[1]
    outs = []
    for j in range(2):
        ls = slice(j * LANE, (j + 1) * LANE)
        q = q_ref[0, :, ls]
        m = jnp.full((tq, 1), NEG, F32)
        acc = jnp.zeros((tq, LANE), F32)
        m, acc = _attn_step(q, kc_ref[0, :, ls], vc_ref[0, :, ls], m, acc)
        m, acc = _attn_step(q, kn_ref[0, :, ls], vn_ref[0, :, ls], m, acc)
        outs.append(_attn_finish(acc))
    o_ref[0] = jnp.concatenate(outs, axis=1)


def _attn_open(q, kc, vc, kn, vn):
    bsz, s, _ = q.shape
    past = kc.shape[1]
    new = pl.BlockSpec((1, s, 2 * LANE), lambda bi, hp: (bi, 0, hp))
    old = pl.BlockSpec((1, past, 2 * LANE), lambda bi, hp: (bi, 0, hp))
    return pl.pallas_call(
        _attn_open_kernel,
        grid=(bsz, N_HEADS // 2),
        in_specs=[new, old, old, new, new],
        out_specs=pl.BlockSpec((1, s, 2 * V_DIM), lambda bi, hp: (bi, 0, hp)),
        out_shape=jax.ShapeDtypeStruct((bsz, s, N_HEADS * V_DIM), F32),
        compiler_params=pltpu.CompilerParams(dimension_semantics=("arbitrary",) * 2, vmem_limit_bytes=VMEM_LIMIT),
        name="attn_open",
    )(q, kc, vc, kn, vn)


def _post_kernel(x_ref, attn_ref, conv_ref, p_ref, ga_ref, wout_ref, gffn_ref, wup_ref, wdn_ref, gple_ref,
                 wgate_ref, wproj_ref, gfin_ref, y_ref, *, ff_chunk):
    aw = attn_ref.shape[1]
    a = _rms(attn_ref[...], ga_ref[...]).astype(BF16)
    x = x_ref[...] + _dot(a, wout_ref[:aw, :]) + _dot(conv_ref[...].astype(BF16), wout_ref[aw:, :])
    h = _rms(x, gffn_ref[...]).astype(BF16)
    ff = None
    for c in range(wup_ref.shape[1] // ff_chunk):
        cs = slice(c * ff_chunk, (c + 1) * ff_chunk)
        f = jnp.maximum(_dot(h, wup_ref[:, cs]), 0.0)
        part = _dot((f * f).astype(BF16), wdn_ref[cs, :])
        ff = part if ff is None else ff + part
    x = x + ff
    gate = jax.nn.sigmoid(_dot(_rms(x, gple_ref[...]).astype(BF16), wgate_ref[...]))
    x = x + gate * _dot(p_ref[...].astype(BF16), wproj_ref[...])
    y_ref[...] = _rms(x, gfin_ref[...])


def _post(x, attn, conv, p, ga, wout, gffn, wup, wdn, gple, wgate, wproj, gfin, tm):
    n, d = x.shape
    row = lambda w: pl.BlockSpec((tm, w), lambda i: (i, 0))
    return pl.pallas_call(
        functools.partial(_post_kernel, ff_chunk=d),
        grid=(n // tm,),
        in_specs=[row(d), row(attn.shape[1]), row(conv.shape[1]), row(p.shape[1]), _full(ga.shape),
                  _resident(wout.shape), _full(gffn.shape), _resident(wup.shape), _resident(wdn.shape),
                  _full(gple.shape), _resident(wgate.shape), _resident(wproj.shape), _full(gfin.shape)],
        out_specs=row(d),
        out_shape=jax.ShapeDtypeStruct((n, d), F32),
        compiler_params=pltpu.CompilerParams(dimension_semantics=("arbitrary",), vmem_limit_bytes=VMEM_LIMIT),
        name="post",
    )(x, attn, conv, p, ga, wout, gffn, wup, wdn, gple, wgate, wproj, gfin)


def _swap_halves(w):
    half = w.shape[-1] // 2
    return jnp.concatenate([-w[..., half:], w[..., :half]], axis=-1)


def _prep_weights(w_in, w_uq, w_ukv):
    d = w_in.shape[0]
    o_kr = Q_LORA + KV_LORA
    w_kr = w_in[:, o_kr:o_kr + QK_ROPE]
    w_kr_sw = _swap_halves(w_kr)
    win = jnp.concatenate([w_in[:, :o_kr], w_in[:, o_kr + QK_ROPE:], w_kr, w_kr_sw, w_kr, w_kr_sw], axis=1)

    pad = LANE - QK_DIM
    w3 = w_uq.reshape(Q_LORA, N_HEADS, QK_DIM)
    rope = w3[..., QK_NOPE:]
    wa = jnp.concatenate([w3, jnp.zeros((Q_LORA, N_HEADS, pad), F32)], axis=-1)
    wb = jnp.concatenate([jnp.zeros((Q_LORA, N_HEADS, QK_NOPE), F32), _swap_halves(rope),
                          jnp.zeros((Q_LORA, N_HEADS, pad), F32)], axis=-1)
    wuq = jnp.concatenate([wa.reshape(Q_LORA, HEAD_W), wb.reshape(Q_LORA, HEAD_W)], axis=1)

    kv3 = w_ukv.reshape(KV_LORA, N_HEADS, QK_NOPE + V_DIM)
    wk = jnp.concatenate([kv3[..., :QK_NOPE], jnp.zeros((KV_LORA, N_HEADS, LANE - QK_NOPE), F32)], axis=-1)
    wv = jnp.concatenate([kv3[..., QK_NOPE:], jnp.zeros((KV_LORA, N_HEADS, LANE - V_DIM), F32)], axis=-1)
    wkv = jnp.concatenate([wk.reshape(KV_LORA, HEAD_W), wv.reshape(KV_LORA, HEAD_W)], axis=1)
    del d
    return win.astype(BF16), wuq.astype(BF16), wkv.astype(BF16)


def _rope_tables(pos):
    half = QK_ROPE // 2
    inv = ROPE_BASE ** (-jnp.arange(half, dtype=F32) / half)
    ang = pos.astype(F32)[:, None] * inv[None, :]
    cos, sin = jnp.cos(ang), jnp.sin(ang)
    cos2 = jnp.concatenate([cos, cos], axis=1)
    sin2 = jnp.concatenate([sin, sin], axis=1)
    n = pos.shape[0]
    scale = QK_DIM ** -0.5
    tqc = scale * jnp.concatenate([jnp.ones((n, QK_NOPE), F32), cos2, jnp.zeros((n, LANE - QK_DIM), F32)], axis=1)
    tqs = scale * jnp.concatenate([jnp.zeros((n, QK_NOPE), F32), sin2, jnp.zeros((n, LANE - QK_DIM), F32)], axis=1)
    tk = jnp.concatenate([cos2, sin2, cos2, sin2], axis=1)
    return tqc, tqs, tk


def _row_tile(n, want):
    t = min(n, want)
    while n % t:
        t //= 2
    return t


def _layer(x, p, pos, past, lw, fin_g):
    bsz, s, d = x.shape
    n = bsz * s
    tm = _row_tile(n if past is not None else s, 512)
    tqc, tqs, tk = _rope_tables(pos)
    if past is not None:
        tqc, tqs, tk = (jnp.tile(t, (bsz, 1)) for t in (tqc, tqs, tk))
    q, k, v, ckv, kr, u = _inproj(x.reshape(n, d), tqc, tqs, tk, lw["gmix"], lw["win"], lw["gq"], lw["wuq"],
                                  lw["gkv"], lw["wkv"], tm)
    u3 = u.reshape(bsz, s, CONV_CH)
    if past is None:
        conv_past = jnp.zeros((bsz, HALO, CONV_CH), F32)
    else:
        conv_past = jnp.pad(past[2], ((0, 0), (HALO - (CONV_K - 1), 0), (0, 0)))
    conv = _conv(u3, conv_past, lw["conv_w"], lw["conv_b"], lw["ln_g"], lw["ln_b"], lw["conv_og"],
                 _row_tile(s, 512))
    q3, k3, v3 = (t.reshape(bsz, s, HEAD_W) for t in (q, k, v))
    if past is None:
        tq = _row_tile(s, 512)
        attn = _attn_causal(q3, k3, v3, tq, tq)
    else:
        plen = past[0].shape[1]
        krp = jnp.pad(past[1].reshape(bsz * plen, QK_ROPE), ((0, 0), (QK_NOPE, LANE - QK_DIM)))
        kc, vc = _kvexpand(past[0].reshape(bsz * plen, KV_LORA), krp, lw["wkv"], _row_tile(bsz * plen, 512))
        attn = _attn_open(q3, kc.reshape(bsz, plen, HEAD_W), vc.reshape(bsz, plen, HEAD_W), k3, v3)
    y = _post(x.reshape(n, d), attn.reshape(n, -1), conv.reshape(n, CONV_CH), p.reshape(n, -1), lw["ga"],
              lw["wout"], lw["gffn"], lw["wup"], lw["wdn"], lw["gple"], lw["wgate"], lw["wproj"], fin_g,
              _row_tile(n, 512))
    assert s >= CONV_K - 1, "the new conv state is taken from this step's rows only"
    conv_new = u3[:, s - (CONV_K - 1):]
    return y.reshape(bsz, s, d), ckv.reshape(bsz, s, KV_LORA), kr.reshape(bsz, s, QK_ROPE), conv_new


def kernel(x_prompt, x_sample, cache_ckv, cache_krope, state_conv, p_prompt, p_sample, norm_mix_g, w_in, q_norm_g, w_uq, kv_norm_g, w_ukv, conv_w, conv_b, conv_ln_g, conv_ln_b, attn_out_g, conv_out_g, w_out, norm_ffn_g, w_ff_up, w_ff_down, norm_ple_g, w_ple_gate, w_ple_proj, norm_final_g):
    depth = w_in.shape[0]
    assert depth == 1, "the final RMSNorm is fused into the (single) layer's last kernel"
    past_len = cache_ckv.shape[2]
    pos_prompt = jnp.arange(x_prompt.shape[1])
    pos_sample = past_len + jnp.arange(x_sample.shape[1])
    row = lambda g: g.reshape(1, -1)
    outs_p, outs_s = [], []
    hp, hs = x_prompt, x_sample
    for i in range(depth):
        win, wuq, wkv = _prep_weights(w_in[i], w_uq[i], w_ukv[i])
        lw = dict(
            gmix=row(norm_mix_g[i]), win=win, gq=row(q_norm_g[i]), wuq=wuq, gkv=row(kv_norm_g[i]), wkv=wkv,
            conv_w=jnp.pad(conv_w[i], ((0, HALO - CONV_K), (0, 0))), conv_b=row(conv_b[i]),
            ln_g=row(conv_ln_g[i]), ln_b=row(conv_ln_b[i]), conv_og=row(conv_out_g[i]),
            ga=row(attn_out_g[i]), wout=w_out[i].astype(BF16), gffn=row(norm_ffn_g[i]),
            wup=w_ff_up[i].astype(BF16), wdn=w_ff_down[i].astype(BF16), gple=row(norm_ple_g[i]),
            wgate=w_ple_gate[i].astype(BF16), wproj=w_ple_proj[i].astype(BF16))
        fin = row(norm_final_g)
        hp, a, b, c = _layer(hp, p_prompt[i], pos_prompt, None, lw, fin)
        outs_p.append((a, b, c))
        hs, a, b, c = _layer(hs, p_sample[i], pos_sample, (cache_ckv[i], cache_krope[i], state_conv[i]), lw, fin)
        outs_s.append((a, b, c))
    stack = lambda outs, j: jnp.stack([o[j] for o in outs])
    return (hp, hs, stack(outs_p, 0), stack(outs_p, 1), stack(outs_p, 2),
            stack(outs_s, 0), stack(outs_s, 1), stack(outs_s, 2))
```

```python
import functools

import jax
import jax.numpy as jnp
from jax import lax
from jax.experimental import pallas as pl
from jax.experimental.pallas import tpu as pltpu

F32 = jnp.float32
BF16 = jnp.bfloat16

CHUNK = 64
N_HEADS = 8
QK_NOPE = 64
QK_ROPE = 32
V_DIM = 64
Q_LORA = 384
KV_LORA = 256
ROPE_BASE = 10000.0
CONV_CH = 512
CONV_K = 31
QK_DIM = QK_NOPE + QK_ROPE
EPS = 1e-6
LN_EPS = 1e-5
NEG = -1e30

LANE = 128
SUBLANE = 8
HEAD_W = N_HEADS * LANE
HALO = 32
CONV_ROWS = 32
SOFTMAX_ROWS = 128
VMEM_LIMIT = 56 * 1024 * 1024


def _rms(x, g):
    return x * lax.rsqrt(jnp.mean(x * x, axis=-1, keepdims=True) + EPS) * g


def _dot(a, b):
    return jnp.dot(a, b, preferred_element_type=F32)


def _full(shape):
    return pl.BlockSpec(shape, lambda *_: (0,) * len(shape))


def _resident(shape):
    return pl.BlockSpec(shape, lambda *_: (0,) * len(shape), pipeline_mode=pl.Buffered(1))


def _inproj_kernel(x_ref, tqc_ref, tqs_ref, tk_ref, gmix_ref, win_ref, gq_ref, wuq_ref, gkv_ref, wkv_ref,
                   q_ref, k_ref, v_ref, ckv_ref, kr_ref, u_ref):
    h = _rms(x_ref[...], gmix_ref[...]).astype(BF16)
    z = _dot(h, win_ref[...])
    o_kv = Q_LORA
    o_ga = o_kv + KV_LORA
    o_gb = o_ga + CONV_CH
    o_kr = o_gb + CONV_CH

    cq = _rms(z[:, :o_kv], gq_ref[...]).astype(BF16)
    qq = _dot(cq, wuq_ref[...])
    tqc = tqc_ref[...]
    tqs = tqs_ref[...]
    for hd in range(N_HEADS):
        a = qq[:, hd * LANE:(hd + 1) * LANE]
        b = qq[:, HEAD_W + hd * LANE:HEAD_W + (hd + 1) * LANE]
        q_ref[:, hd * LANE:(hd + 1) * LANE] = (a * tqc + b * tqs).astype(BF16)

    ckv = _rms(z[:, o_kv:o_ga], gkv_ref[...])
    ckv_ref[...] = ckv
    kv = _dot(ckv.astype(BF16), wkv_ref[...])

    t = z[:, o_kr:o_kr + LANE] * tk_ref[...]
    krot = t + pltpu.roll(t, QK_ROPE, 1)
    kr_ref[...] = krot[:, :QK_ROPE]
    lane = lax.broadcasted_iota(jnp.int32, krot.shape, 1)
    kr_at_rope = jnp.where((lane >= QK_NOPE) & (lane < QK_DIM), krot, 0.0)
    one_at_sum = jnp.where(lane == V_DIM, 1.0, 0.0)
    for hd in range(N_HEADS):
        k_ref[:, hd * LANE:(hd + 1) * LANE] = (kv[:, hd * LANE:(hd + 1) * LANE] + kr_at_rope).astype(BF16)
        v_ref[:, hd * LANE:(hd + 1) * LANE] = (
            kv[:, HEAD_W + hd * LANE:HEAD_W + (hd + 1) * LANE] + one_at_sum).astype(BF16)

    u_ref[...] = z[:, o_ga:o_gb] * jax.nn.sigmoid(z[:, o_gb:o_kr])


def _inproj(x, tqc, tqs, tk, gmix, win, gq, wuq, gkv, wkv, tm):
    n, d = x.shape
    nt = tqc.shape[0] // tm
    row = lambda w: pl.BlockSpec((tm, w), lambda i: (i, 0))
    tab = pl.BlockSpec((tm, LANE), lambda i: (i % nt, 0))
    return pl.pallas_call(
        _inproj_kernel,
        grid=(n // tm,),
        in_specs=[row(d), tab, tab, tab, _full(gmix.shape), _resident(win.shape), _full(gq.shape),
                  _resident(wuq.shape), _full(gkv.shape), _resident(wkv.shape)],
        out_specs=[row(HEAD_W), row(HEAD_W), row(HEAD_W), row(KV_LORA), row(QK_ROPE), row(CONV_CH)],
        out_shape=[jax.ShapeDtypeStruct((n, HEAD_W), BF16), jax.ShapeDtypeStruct((n, HEAD_W), BF16),
                   jax.ShapeDtypeStruct((n, HEAD_W), BF16), jax.ShapeDtypeStruct((n, KV_LORA), F32),
                   jax.ShapeDtypeStruct((n, QK_ROPE), F32), jax.ShapeDtypeStruct((n, CONV_CH), F32)],
        compiler_params=pltpu.CompilerParams(dimension_semantics=("arbitrary",), vmem_limit_bytes=VMEM_LIMIT),
        name="inproj",
    )(x, tqc, tqs, tk, gmix, win, gq, wuq, gkv, wkv)


def _kvexpand_kernel(ckv_ref, krp_ref, wkv_ref, k_ref, v_ref):
    kv = _dot(ckv_ref[...].astype(BF16), wkv_ref[...])
    krp = krp_ref[...]
    lane = lax.broadcasted_iota(jnp.int32, krp.shape, 1)
    one_at_sum = jnp.where(lane == V_DIM, 1.0, 0.0)
    for hd in range(N_HEADS):
        k_ref[:, hd * LANE:(hd + 1) * LANE] = (kv[:, hd * LANE:(hd + 1) * LANE] + krp).astype(BF16)
        v_ref[:, hd * LANE:(hd + 1) * LANE] = (
            kv[:, HEAD_W + hd * LANE:HEAD_W + (hd + 1) * LANE] + one_at_sum).astype(BF16)


def _kvexpand(ckv, krp, wkv, tm):
    n = ckv.shape[0]
    row = lambda w: pl.BlockSpec((tm, w), lambda i: (i, 0))
    return pl.pallas_call(
        _kvexpand_kernel,
        grid=(n // tm,),
        in_specs=[row(KV_LORA), row(LANE), _resident(wkv.shape)],
        out_specs=[row(HEAD_W), row(HEAD_W)],
        out_shape=[jax.ShapeDtypeStruct((n, HEAD_W), BF16), jax.ShapeDtypeStruct((n, HEAD_W), BF16)],
        compiler_params=pltpu.CompilerParams(dimension_semantics=("arbitrary",), vmem_limit_bytes=VMEM_LIMIT),
        name="kvexpand",
    )(ckv, krp, wkv)


def _conv_kernel(u_ref, uprev_ref, past_ref, w_ref, b_ref, lng_ref, lnb_ref, og_ref, o_ref, ext_ref, sh_ref, *, tc):
    first = pl.program_id(1) == 0
    ext_ref[0:HALO, :] = jnp.where(first, past_ref[0], uprev_ref[0])
    ext_ref[HALO:, :] = u_ref[0]
    span = tc + HALO - SUBLANE
    for d in range(1, SUBLANE):
        sh_ref[d - 1, 0:span, :] = ext_ref[d:d + span, :]
    lead = HALO - (CONV_K - 1)
    for r in range(tc // CONV_ROWS):
        acc = jnp.broadcast_to(b_ref[...], (CONV_ROWS, CONV_CH))
        for j in range(CONV_K):
            whole, d = divmod(lead + j, SUBLANE)
            src = ext_ref if d == 0 else sh_ref.at[d - 1]
            at = r * CONV_ROWS + whole * SUBLANE
            wj = jnp.tile(w_ref[j * SUBLANE:(j + 1) * SUBLANE, :], (CONV_ROWS // SUBLANE, 1))
            acc = acc + wj * src[at:at + CONV_ROWS, :]
        mu = jnp.mean(acc, axis=-1, keepdims=True)
        dlt = acc - mu
        var = jnp.mean(dlt * dlt, axis=-1, keepdims=True)
        y = dlt * lax.rsqrt(var + LN_EPS) * lng_ref[...] + lnb_ref[...]
        y = y * jax.nn.sigmoid(y)
        o_ref[0, r * CONV_ROWS:(r + 1) * CONV_ROWS, :] = _rms(y, og_ref[...])


def _conv(u, past, w, b, lng, lnb, og, tc):
    bsz, s, c = u.shape
    per = tc // HALO
    return pl.pallas_call(
        functools.partial(_conv_kernel, tc=tc),
        grid=(bsz, s // tc),
        in_specs=[pl.BlockSpec((1, tc, c), lambda bi, i: (bi, i, 0)),
                  pl.BlockSpec((1, HALO, c), lambda bi, i: (bi, jnp.maximum(i * per - 1, 0), 0)),
                  pl.BlockSpec((1, HALO, c), lambda bi, i: (bi, 0, 0)),
                  _full(w.shape), _full(b.shape), _full(lng.shape), _full(lnb.shape), _full(og.shape)],
        out_specs=pl.BlockSpec((1, tc, c), lambda bi, i: (bi, i, 0)),
        out_shape=jax.ShapeDtypeStruct((bsz, s, c), F32),
        scratch_shapes=[pltpu.VMEM((tc + HALO, c), F32), pltpu.VMEM((SUBLANE - 1, tc + HALO, c), F32)],
        compiler_params=pltpu.CompilerParams(dimension_semantics=("arbitrary", "arbitrary"),
                                             vmem_limit_bytes=VMEM_LIMIT),
        name="convmod",
    )(u, u, past, w, b, lng, lnb, og)


def _scores(q, kblk):
    return lax.dot_general(q, kblk, (((1,), (1,)), ((), ())), preferred_element_type=F32)


def _softmax_pv(s, vblk, m, acc, mask=None):
    if mask is not None:
        s = jnp.where(mask, s, NEG)
    m_new = jnp.maximum(m, jnp.max(s, axis=-1, keepdims=True))
    p = jnp.exp2((s - m_new).astype(BF16))
    acc = jnp.exp2(m - m_new) * acc + _dot(p, vblk)
    return m_new, acc


def _attn_step(q, kblk, vblk, m, acc, mask=None):
    return _softmax_pv(_scores(q, kblk), vblk, m, acc, mask)


def _attn_finish(acc):
    return (acc / acc[:, V_DIM:V_DIM + 1])[:, :V_DIM]


def _attn_causal_kernel(q_ref, k_ref, v_ref, o_ref, sa_ref, sb_ref, p_ref, m_ref, acc_ref, *, tq, tk):
    qi = pl.program_id(2)
    lanes = [slice(j * LANE, (j + 1) * LANE) for j in range(2)]
    qs = [q_ref[0, :, ls] for ls in lanes]
    blk = lambda i: pl.multiple_of(i * tk, tk)
    low = slice(tk, tq)

    def scores(i, s_ref, rows=slice(None)):
        for j, ls in enumerate(lanes):
            s_ref[j, rows, :] = _scores(qs[j][rows], k_ref[0, pl.ds(blk(i), tk), ls])

    def update(s_ref, i, rows=slice(0, tq), mask=None):
        for j, ls in enumerate(lanes):
            for r0 in range(rows.start, rows.stop, SOFTMAX_ROWS):
                rc = slice(r0, r0 + SOFTMAX_ROWS)
                s = s_ref[j, rc, :]
                if mask is not None:
                    s = jnp.where(mask[r0 - rows.start:r0 - rows.start + SOFTMAX_ROWS], s, NEG)
                m_old = m_ref[j, rc, :]
                m_new = jnp.maximum(m_old, jnp.max(s, axis=-1, keepdims=True))
                m_ref[j, rc, :] = m_new
                acc_ref[j, rc, :] = acc_ref[j, rc, :] * jnp.exp2(m_old - m_new)
                m_wide = jnp.concatenate([m_new] * (tk // LANE), axis=1)
                p_ref[j, rc, :] = jnp.exp2((s - m_wide).astype(BF16))
            acc_ref[j, rows, :] += _dot(p_ref[j, rows, :], v_ref[0, pl.ds(blk(i), tk), ls])

    def pair(p, carry):
        i = 2 * p
        scores(i + 1, sb_ref)
        update(sa_ref, i)
        scores(i + 2, sa_ref)
        update(sb_ref, i + 1)
        return carry

    def finish(rows):
        o_ref[0, rows, :] = jnp.concatenate([_attn_finish(acc_ref[j, rows, :]) for j in range(2)], axis=1)

    m_ref[...] = jnp.full(m_ref.shape, NEG, F32)
    acc_ref[...] = jnp.zeros(acc_ref.shape, F32)
    n_full = qi * (tq // tk)
    scores(0, sa_ref)
    lax.fori_loop(0, n_full // 2, pair, 0)
    scores(n_full + 1, sb_ref, low)
    chunk_of = lambda shape, axis: lax.broadcasted_iota(jnp.int32, shape, axis) // CHUNK
    update(sa_ref, n_full, mask=chunk_of((tq, tk), 1) <= chunk_of((tq, tk), 0))
    finish(slice(0, tk))
    update(sb_ref, n_full + 1, low, chunk_of((tk, tk), 1) <= chunk_of((tk, tk), 0))
    finish(low)


def _attn_causal(q, k, v, tq, tk):
    assert tq == 2 * tk
    bsz, s, _ = q.shape
    kv_spec = pl.BlockSpec((1, s, 2 * LANE), lambda bi, hp, qi: (bi, 0, hp))
    return pl.pallas_call(
        functools.partial(_attn_causal_kernel, tq=tq, tk=tk),
        grid=(bsz, N_HEADS // 2, s // tq),
        in_specs=[pl.BlockSpec((1, tq, 2 * LANE), lambda bi, hp, qi: (bi, qi, hp)), kv_spec, kv_spec],
        out_specs=pl.BlockSpec((1, tq, 2 * V_DIM), lambda bi, hp, qi: (bi, qi, hp)),
        out_shape=jax.ShapeDtypeStruct((bsz, s, N_HEADS * V_DIM), F32),
        scratch_shapes=[pltpu.VMEM((2, tq, tk), F32), pltpu.VMEM((2, tq, tk), F32), pltpu.VMEM((2, tq, tk), BF16),
                        pltpu.VMEM((2, tq, LANE), F32), pltpu.VMEM((2, tq, LANE), F32)],
        compiler_params=pltpu.CompilerParams(dimension_semantics=("arbitrary",) * 3, vmem_limit_bytes=VMEM_LIMIT),
        name="attn_causal",
    )(q, k, v)


def _attn_open_kernel(q_ref, kc_ref, vc_ref, kn_ref, vn_ref, o_ref):
    tq = q_ref.shape[1]
    outs = []
    for j in range(2):
        ls = slice(j * LANE, (j + 1) * LANE)
        q = q_ref[0, :, ls]
        m = jnp.full((tq, 1), NEG, F32)
        acc = jnp.zeros((tq, LANE), F32)
        m, acc = _attn_step(q, kc_ref[0, :, ls], vc_ref[0, :, ls], m, acc)
        m, acc = _attn_step(q, kn_ref[0, :, ls], vn_ref[0, :, ls], m, acc)
        outs.append(_attn_finish(acc))
    o_ref[0] = jnp.concatenate(outs, axis=1)


def _attn_open(q, kc, vc, kn, vn):
    bsz, s, _ = q.shape
    past = kc.shape[1]
    new = pl.BlockSpec((1, s, 2 * LANE), lambda bi, hp: (bi, 0, hp))
    old = pl.BlockSpec((1, past, 2 * LANE), lambda bi, hp: (bi, 0, hp))
    return pl.pallas_call(
        _attn_open_kernel,
        grid=(bsz, N_HEADS // 2),
        in_specs=[new, old, old, new, new],
        out_specs=pl.BlockSpec((1, s, 2 * V_DIM), lambda bi, hp: (bi, 0, hp)),
        out_shape=jax.ShapeDtypeStruct((bsz, s, N_HEADS * V_DIM), F32),
        compiler_params=pltpu.CompilerParams(dimension_semantics=("arbitrary",) * 2, vmem_limit_bytes=VMEM_LIMIT),
        name="attn_open",
    )(q, kc, vc, kn, vn)


def _post_kernel(x_ref, attn_ref, conv_ref, p_ref, ga_ref, wout_ref, gffn_ref, wup_ref, wdn_ref, gple_ref,
                 wgate_ref, wproj_ref, gfin_ref, y_ref, *, ff_chunk):
    aw = attn_ref.shape[1]
    a = _rms(attn_ref[...], ga_ref[...]).astype(BF16)
    x = x_ref[...] + _dot(a, wout_ref[:aw, :]) + _dot(conv_ref[...].astype(BF16), wout_ref[aw:, :])
    h = _rms(x, gffn_ref[...]).astype(BF16)
    ff = None
    for c in range(wup_ref.shape[1] // ff_chunk):
        cs = slice(c * ff_chunk, (c + 1) * ff_chunk)
        f = jnp.maximum(_dot(h, wup_ref[:, cs]), 0.0)
        part = _dot((f * f).astype(BF16), wdn_ref[cs, :])
        ff = part if ff is None else ff + part
    x = x + ff
    gate = jax.nn.sigmoid(_dot(_rms(x, gple_ref[...]).astype(BF16), wgate_ref[...]))
    x = x + gate * _dot(p_ref[...].astype(BF16), wproj_ref[...])
    y_ref[...] = _rms(x, gfin_ref[...])


def _post(x, attn, conv, p, ga, wout, gffn, wup, wdn, gple, wgate, wproj, gfin, tm):
    n, d = x.shape
    row = lambda w: pl.BlockSpec((tm, w), lambda i: (i, 0))
    return pl.pallas_call(
        functools.partial(_post_kernel, ff_chunk=d),
        grid=(n // tm,),
        in_specs=[row(d), row(attn.shape[1]), row(conv.shape[1]), row(p.shape[1]), _full(ga.shape),
                  _resident(wout.shape), _full(gffn.shape), _resident(wup.shape), _resident(wdn.shape),
                  _full(gple.shape), _resident(wgate.shape), _resident(wproj.shape), _full(gfin.shape)],
        out_specs=row(d),
        out_shape=jax.ShapeDtypeStruct((n, d), F32),
        compiler_params=pltpu.CompilerParams(dimension_semantics=("arbitrary",), vmem_limit_bytes=VMEM_LIMIT),
        name="post",
    )(x, attn, conv, p, ga, wout, gffn, wup, wdn, gple, wgate, wproj, gfin)


def _swap_halves(w):
    half = w.shape[-1] // 2
    return jnp.concatenate([-w[..., half:], w[..., :half]], axis=-1)


def _prep_weights(w_in, w_uq, w_ukv):
    d = w_in.shape[0]
    o_kr = Q_LORA + KV_LORA
    w_kr = w_in[:, o_kr:o_kr + QK_ROPE]
    w_kr_sw = _swap_halves(w_kr)
    win = jnp.concatenate([w_in[:, :o_kr], w_in[:, o_kr + QK_ROPE:], w_kr, w_kr_sw, w_kr, w_kr_sw], axis=1)

    pad = LANE - QK_DIM
    w3 = w_uq.reshape(Q_LORA, N_HEADS, QK_DIM)
    rope = w3[..., QK_NOPE:]
    wa = jnp.concatenate([w3, jnp.zeros((Q_LORA, N_HEADS, pad), F32)], axis=-1)
    wb = jnp.concatenate([jnp.zeros((Q_LORA, N_HEADS, QK_NOPE), F32), _swap_halves(rope),
                          jnp.zeros((Q_LORA, N_HEADS, pad), F32)], axis=-1)
    wuq = jnp.concatenate([wa.reshape(Q_LORA, HEAD_W), wb.reshape(Q_LORA, HEAD_W)], axis=1)

    kv3 = w_ukv.reshape(KV_LORA, N_HEADS, QK_NOPE + V_DIM)
    wk = jnp.concatenate([kv3[..., :QK_NOPE], jnp.zeros((KV_LORA, N_HEADS, LANE - QK_NOPE), F32)], axis=-1)
    wv = jnp.concatenate([kv3[..., QK_NOPE:], jnp.zeros((KV_LORA, N_HEADS, LANE - V_DIM), F32)], axis=-1)
    wkv = jnp.concatenate([wk.reshape(KV_LORA, HEAD_W), wv.reshape(KV_LORA, HEAD_W)], axis=1)
    del d
    return win.astype(BF16), wuq.astype(BF16), wkv.astype(BF16)


def _rope_tables(pos):
    half = QK_ROPE // 2
    inv = ROPE_BASE ** (-jnp.arange(half, dtype=F32) / half)
    ang = pos.astype(F32)[:, None] * inv[None, :]
    cos, sin = jnp.cos(ang), jnp.sin(ang)
    cos2 = jnp.concatenate([cos, cos], axis=1)
    sin2 = jnp.concatenate([sin, sin], axis=1)
    n = pos.shape[0]
    scale = QK_DIM ** -0.5 * 1.4426950408889634
    tqc = scale * jnp.concatenate([jnp.ones((n, QK_NOPE), F32), cos2, jnp.zeros((n, LANE - QK_DIM), F32)], axis=1)
    tqs = scale * jnp.concatenate([jnp.zeros((n, QK_NOPE), F32), sin2, jnp.zeros((n, LANE - QK_DIM), F32)], axis=1)
    tk = jnp.concatenate([cos2, sin2, cos2, sin2], axis=1)
    return tqc, tqs, tk


def _row_tile(n, want):
    t = min(n, want)
    while n % t:
        t //= 2
    return t


def _layer(x, p, pos, past, lw, fin_g):
    bsz, s, d = x.shape
    n = bsz * s
    tm = _row_tile(n if past is not None else s, 512)
    tqc, tqs, tk = _rope_tables(pos)
    if past is not None:
        tqc, tqs, tk = (jnp.tile(t, (bsz, 1)) for t in (tqc, tqs, tk))
    q, k, v, ckv, kr, u = _inproj(x.reshape(n, d), tqc, tqs, tk, lw["gmix"], lw["win"], lw["gq"], lw["wuq"],
                                  lw["gkv"], lw["wkv"], tm)
    u3 = u.reshape(bsz, s, CONV_CH)
    if past is None:
        conv_past = jnp.zeros((bsz, HALO, CONV_CH), F32)
    else:
        conv_past = jnp.pad(past[2], ((0, 0), (HALO - (CONV_K - 1), 0), (0, 0)))
    conv = _conv(u3, conv_past, lw["conv_w"], lw["conv_b"], lw["ln_g"], lw["ln_b"], lw["conv_og"],
                 _row_tile(s, 512))
    q3, k3, v3 = (t.reshape(bsz, s, HEAD_W) for t in (q, k, v))
    if past is None:
        attn = _attn_causal(q3, k3, v3, _row_tile(s, 1024), _row_tile(s, 512))
    else:
        plen = past[0].shape[1]
        krp = jnp.pad(past[1].reshape(bsz * plen, QK_ROPE), ((0, 0), (QK_NOPE, LANE - QK_DIM)))
        kc, vc = _kvexpand(past[0].reshape(bsz * plen, KV_LORA), krp, lw["wkv"], _row_tile(bsz * plen, 512))
        attn = _attn_open(q3, kc.reshape(bsz, plen, HEAD_W), vc.reshape(bsz, plen, HEAD_W), k3, v3)
    y = _post(x.reshape(n, d), attn.reshape(n, -1), conv.reshape(n, CONV_CH), p.reshape(n, -1), lw["ga"],
              lw["wout"], lw["gffn"], lw["wup"], lw["wdn"], lw["gple"], lw["wgate"], lw["wproj"], fin_g,
              _row_tile(n, 512))
    assert s >= CONV_K - 1, "the new conv state is taken from this step's rows only"
    conv_new = u3[:, s - (CONV_K - 1):]
    return y.reshape(bsz, s, d), ckv.reshape(bsz, s, KV_LORA), kr.reshape(bsz, s, QK_ROPE), conv_new


def kernel(x_prompt, x_sample, cache_ckv, cache_krope, state_conv, p_prompt, p_sample, norm_mix_g, w_in, q_norm_g, w_uq, kv_norm_g, w_ukv, conv_w, conv_b, conv_ln_g, conv_ln_b, attn_out_g, conv_out_g, w_out, norm_ffn_g, w_ff_up, w_ff_down, norm_ple_g, w_ple_gate, w_ple_proj, norm_final_g):
    depth = w_in.shape[0]
    assert depth == 1, "the final RMSNorm is fused into the (single) layer's last kernel"
    past_len = cache_ckv.shape[2]
    pos_prompt = jnp.arange(x_prompt.shape[1])
    pos_sample = past_len + jnp.arange(x_sample.shape[1])
    row = lambda g: g.reshape(1, -1)
    outs_p, outs_s = [], []
    hp, hs = x_prompt, x_sample
    for i in range(depth):
        win, wuq, wkv = _prep_weights(w_in[i], w_uq[i], w_ukv[i])
        lw = dict(
            gmix=row(norm_mix_g[i]), win=win, gq=row(q_norm_g[i]), wuq=wuq, gkv=row(kv_norm_g[i]), wkv=wkv,
            conv_w=jnp.repeat(conv_w[i], SUBLANE, axis=0), conv_b=row(conv_b[i]),
            ln_g=row(conv_ln_g[i]), ln_b=row(conv_ln_b[i]), conv_og=row(conv_out_g[i]),
            ga=row(attn_out_g[i]), wout=w_out[i].astype(BF16), gffn=row(norm_ffn_g[i]),
            wup=w_ff_up[i].astype(BF16), wdn=w_ff_down[i].astype(BF16), gple=row(norm_ple_g[i]),
            wgate=w_ple_gate[i].astype(BF16), wproj=w_ple_proj[i].astype(BF16))
        fin = row(norm_final_g)
        hp, a, b, c = _layer(hp, p_prompt[i], pos_prompt, None, lw, fin)
        outs_p.append((a, b, c))
        hs, a, b, c = _layer(hs, p_sample[i], pos_sample, (cache_ckv[i], cache_krope[i], state_conv[i]), lw, fin)
        outs_s.append((a, b, c))
    stack = lambda outs, j: jnp.stack([o[j] for o in outs])
    return (hp, hs, stack(outs_p, 0), stack(outs_p, 1), stack(outs_p, 2),
            stack(outs_s, 0), stack(outs_s, 1), stack(outs_s, 2))
```

```python
import functools

import jax
import jax.numpy as jnp
from jax import lax
from jax.experimental import pallas as pl
from jax.experimental.pallas import tpu as pltpu

F32 = jnp.float32
BF16 = jnp.bfloat16

CHUNK = 64
N_HEADS = 8
QK_NOPE = 64
QK_ROPE = 32
V_DIM = 64
Q_LORA = 384
KV_LORA = 256
ROPE_BASE = 10000.0
CONV_CH = 512
CONV_K = 31
QK_DIM = QK_NOPE + QK_ROPE
EPS = 1e-6
LN_EPS = 1e-5
NEG = -1e30

LANE = 128
SUBLANE = 8
HEAD_W = N_HEADS * LANE
HALO = 32
CONV_ROWS = 32
SOFTMAX_ROWS = 128
FF_CHUNK = 1024
ROW_TILE = 512
ATTN_Q_ROWS = 1024
VMEM_LIMIT = 56 * 1024 * 1024


def _rms(x, g):
    return x * lax.rsqrt(jnp.mean(x * x, axis=-1, keepdims=True) + EPS) * g


def _dot(a, b):
    return jnp.dot(a, b, preferred_element_type=F32)


def _full(shape):
    return pl.BlockSpec(shape, lambda *_: (0,) * len(shape))


def _resident(shape):
    return pl.BlockSpec(shape, lambda *_: (0,) * len(shape), pipeline_mode=pl.Buffered(1))


def _conv_shifts(ext_ref, sh_ref, tc):
    for d in range(SUBLANE):
        span = tc + HALO - (SUBLANE if d else 0)
        sh_ref[d, 0:span, :] = ext_ref[d:d + span, :]


def _conv_rows(sh_ref, w_ref, b_ref, lng_ref, lnb_ref, og_ref, o_ref, row_blocks):
    lead = HALO - (CONV_K - 1)
    for r in row_blocks:
        row0 = r * CONV_ROWS
        acc = jnp.broadcast_to(b_ref[...], (CONV_ROWS, CONV_CH))
        for j in range(CONV_K):
            whole, d = divmod(lead + j, SUBLANE)
            wj = jnp.tile(w_ref[j * SUBLANE:(j + 1) * SUBLANE, :], (CONV_ROWS // SUBLANE, 1))
            acc = acc + wj * sh_ref[d, pl.ds(row0 + whole * SUBLANE, CONV_ROWS), :]
        mu = jnp.mean(acc, axis=-1, keepdims=True)
        dlt = acc - mu
        var = jnp.mean(dlt * dlt, axis=-1, keepdims=True)
        y = dlt * lax.rsqrt(var + LN_EPS) * lng_ref[...] + lnb_ref[...]
        y = y * jax.nn.sigmoid(y)
        o_ref[pl.ds(row0, CONV_ROWS), :] = _rms(y, og_ref[...]).astype(o_ref.dtype)


def _inproj_kernel(x_ref, tqc_ref, tqs_ref, tk_ref, past_ref, gmix_ref, win_ref, gq_ref, wuq_ref, gkv_ref, wkv_ref,
                   cw_ref, cb_ref, lng_ref, lnb_ref, og_ref,
                   q_ref, k_ref, v_ref, ckv_ref, kr_ref, conv_ref, tail_ref, ext_ref, sh_ref, *, tiles_per_seq):
    tm = x_ref.shape[0]
    first = pl.program_id(0) % tiles_per_seq == 0

    @pl.when(first)
    def _():
        ext_ref[0:HALO, :] = past_ref[0]

    @pl.when(jnp.logical_not(first))
    def _():
        ext_ref[0:HALO, :] = ext_ref[tm:tm + HALO, :]

    h = _rms(x_ref[...], gmix_ref[...]).astype(BF16)
    z = _dot(h, win_ref[...])
    o_kv = Q_LORA
    o_ga = o_kv + KV_LORA
    o_gb = o_ga + CONV_CH
    o_kr = o_gb + CONV_CH

    cq = _rms(z[:, :o_kv], gq_ref[...]).astype(BF16)
    qq = _dot(cq, wuq_ref[...])
    tqc = tqc_ref[...]
    tqs = tqs_ref[...]
    for hd in range(N_HEADS):
        a = qq[:, hd * LANE:(hd + 1) * LANE]
        b = qq[:, HEAD_W + hd * LANE:HEAD_W + (hd + 1) * LANE]
        q_ref[:, hd * LANE:(hd + 1) * LANE] = (a * tqc + b * tqs).astype(BF16)

    ckv = _rms(z[:, o_kv:o_ga], gkv_ref[...])
    ckv_ref[...] = ckv
    kv = _dot(ckv.astype(BF16), wkv_ref[...])

    t = z[:, o_kr:o_kr + LANE] * tk_ref[...]
    krot = t + pltpu.roll(t, QK_ROPE, 1)
    kr_ref[...] = krot[:, :QK_ROPE]
    lane = lax.broadcasted_iota(jnp.int32, krot.shape, 1)
    kr_at_rope = jnp.where((lane >= QK_NOPE) & (lane < QK_DIM), krot, 0.0)
    one_at_sum = jnp.where(lane == V_DIM, 1.0, 0.0)
    for hd in range(N_HEADS):
        k_ref[:, hd * LANE:(hd + 1) * LANE] = (kv[:, hd * LANE:(hd + 1) * LANE] + kr_at_rope).astype(BF16)
        v_ref[:, hd * LANE:(hd + 1) * LANE] = (
            kv[:, HEAD_W + hd * LANE:HEAD_W + (hd + 1) * LANE] + one_at_sum).astype(BF16)

    u = z[:, o_ga:o_gb] * jax.nn.sigmoid(z[:, o_gb:o_kr])
    ext_ref[HALO:, :] = u
    tail_ref[0] = u[tm - HALO:, :]
    _conv_shifts(ext_ref, sh_ref, tm)
    _conv_rows(sh_ref, cw_ref, cb_ref, lng_ref, lnb_ref, og_ref, conv_ref, range(tm // CONV_ROWS))


def _inproj(x, tqc, tqs, tk, past, gmix, win, gq, wuq, gkv, wkv, cw, cb, lng, lnb, og, tm):
    n, d = x.shape
    nt = tqc.shape[0] // tm
    row = lambda w: pl.BlockSpec((tm, w), lambda i: (i, 0))
    tab = pl.BlockSpec((tm, LANE), lambda i: (i % nt, 0))
    seq = pl.BlockSpec((1, HALO, CONV_CH), lambda i: (i // nt, 0, 0))
    return pl.pallas_call(
        functools.partial(_inproj_kernel, tiles_per_seq=nt),
        grid=(n // tm,),
        in_specs=[row(d), tab, tab, tab, seq, _full(gmix.shape), _resident(win.shape), _full(gq.shape),
                  _resident(wuq.shape), _full(gkv.shape), _resident(wkv.shape), _full(cw.shape), _full(cb.shape),
                  _full(lng.shape), _full(lnb.shape), _full(og.shape)],
        out_specs=[row(HEAD_W), row(HEAD_W), row(HEAD_W), row(KV_LORA), row(QK_ROPE), row(CONV_CH), seq],
        out_shape=[jax.ShapeDtypeStruct((n, HEAD_W), BF16), jax.ShapeDtypeStruct((n, HEAD_W), BF16),
                   jax.ShapeDtypeStruct((n, HEAD_W), BF16), jax.ShapeDtypeStruct((n, KV_LORA), F32),
                   jax.ShapeDtypeStruct((n, QK_ROPE), F32), jax.ShapeDtypeStruct((n, CONV_CH), BF16),
                   jax.ShapeDtypeStruct(past.shape, F32)],
        scratch_shapes=[pltpu.VMEM((tm + HALO, CONV_CH), F32), pltpu.VMEM((SUBLANE, tm + HALO, CONV_CH), F32)],
        compiler_params=pltpu.CompilerParams(dimension_semantics=("arbitrary",), vmem_limit_bytes=VMEM_LIMIT),
        name="inproj",
    )(x, tqc, tqs, tk, past, gmix, win, gq, wuq, gkv, wkv, cw, cb, lng, lnb, og)


def _kvexpand_kernel(ckv_ref, krp_ref, wkv_ref, k_ref, v_ref):
    kv = _dot(ckv_ref[...].astype(BF16), wkv_ref[...])
    krp = krp_ref[...]
    lane = lax.broadcasted_iota(jnp.int32, krp.shape, 1)
    one_at_sum = jnp.where(lane == V_DIM, 1.0, 0.0)
    for hd in range(N_HEADS):
        k_ref[:, hd * LANE:(hd + 1) * LANE] = (kv[:, hd * LANE:(hd + 1) * LANE] + krp).astype(BF16)
        v_ref[:, hd * LANE:(hd + 1) * LANE] = (
            kv[:, HEAD_W + hd * LANE:HEAD_W + (hd + 1) * LANE] + one_at_sum).astype(BF16)


def _kvexpand(ckv, krp, wkv, tm):
    n = ckv.shape[0]
    row = lambda w: pl.BlockSpec((tm, w), lambda i: (i, 0))
    return pl.pallas_call(
        _kvexpand_kernel,
        grid=(n // tm,),
        in_specs=[row(KV_LORA), row(LANE), _resident(wkv.shape)],
        out_specs=[row(HEAD_W), row(HEAD_W)],
        out_shape=[jax.ShapeDtypeStruct((n, HEAD_W), BF16), jax.ShapeDtypeStruct((n, HEAD_W), BF16)],
        compiler_params=pltpu.CompilerParams(dimension_semantics=("arbitrary",), vmem_limit_bytes=VMEM_LIMIT),
        name="kvexpand",
    )(ckv, krp, wkv)


def _scores(q, kblk):
    return lax.dot_general(q, kblk, (((1,), (1,)), ((), ())), preferred_element_type=F32)


def _softmax_pv(s, vblk, m, acc, mask=None):
    if mask is not None:
        s = jnp.where(mask, s, NEG)
    m_new = jnp.maximum(m, jnp.max(s, axis=-1, keepdims=True))
    p = jnp.exp2((s - m_new).astype(BF16))
    acc = jnp.exp2(m - m_new) * acc + _dot(p, vblk)
    return m_new, acc


def _attn_step(q, kblk, vblk, m, acc, mask=None):
    return _softmax_pv(_scores(q, kblk), vblk, m, acc, mask)


def _attn_finish(acc):
    return (acc / acc[:, V_DIM:V_DIM + 1])[:, :V_DIM]


def _attn_causal_kernel(q_ref, k_ref, v_ref, o_ref, sa_ref, sb_ref, p_ref, m_ref, acc_ref, *, tq, tk):
    qi = pl.program_id(2)
    lanes = [slice(j * LANE, (j + 1) * LANE) for j in range(2)]
    blk = lambda i: pl.multiple_of(i * tk, tk)
    low = slice(tk, tq)

    def scores(i, s_ref, rows=slice(None)):
        for j, ls in enumerate(lanes):
            s_ref[j, rows, :] = _scores(q_ref[0, rows, ls], k_ref[0, pl.ds(blk(i), tk), ls])

    def update(s_ref, i, rows=slice(0, tq), mask=None):
        for j, ls in enumerate(lanes):
            for r0 in range(rows.start, rows.stop, SOFTMAX_ROWS):
                rc = slice(r0, r0 + SOFTMAX_ROWS)
                s = s_ref[j, rc, :]
                if mask is not None:
                    s = jnp.where(mask[r0 - rows.start:r0 - rows.start + SOFTMAX_ROWS], s, NEG)
                m_old = m_ref[j, rc, :]
                m_new = jnp.maximum(m_old, jnp.max(s, axis=-1, keepdims=True))
                m_ref[j, rc, :] = m_new
                acc_ref[j, rc, :] = acc_ref[j, rc, :] * jnp.exp2(m_old - m_new)
                m_wide = jnp.concatenate([m_new] * (tk // LANE), axis=1)
                p_ref[j, rc, :] = jnp.exp2((s - m_wide).astype(BF16))
            acc_ref[j, rows, :] += _dot(p_ref[j, rows, :], v_ref[0, pl.ds(blk(i), tk), ls])

    def pair(p, carry):
        i = 2 * p
        scores(i + 1, sb_ref)
        update(sa_ref, i)
        scores(i + 2, sa_ref)
        update(sb_ref, i + 1)
        return carry

    def finish(rows):
        o_ref[0, rows, :] = jnp.concatenate([_attn_finish(acc_ref[j, rows, :]) for j in range(2)], axis=1)

    m_ref[...] = jnp.full(m_ref.shape, NEG, F32)
    acc_ref[...] = jnp.zeros(acc_ref.shape, F32)
    n_full = qi * (tq // tk)
    scores(0, sa_ref)
    lax.fori_loop(0, n_full // 2, pair, 0)
    scores(n_full + 1, sb_ref, low)
    chunk_of = lambda shape, axis: lax.broadcasted_iota(jnp.int32, shape, axis) // CHUNK
    update(sa_ref, n_full, mask=chunk_of((tq, tk), 1) <= chunk_of((tq, tk), 0))
    finish(slice(0, tk))
    update(sb_ref, n_full + 1, low, chunk_of((tk, tk), 1) <= chunk_of((tk, tk), 0))
    finish(low)


def _attn_causal(q, k, v, tq, tk):
    assert tq == 2 * tk
    bsz, s, _ = q.shape
    kv_spec = pl.BlockSpec((1, s, 2 * LANE), lambda bi, hp, qi: (bi, 0, hp))
    return pl.pallas_call(
        functools.partial(_attn_causal_kernel, tq=tq, tk=tk),
        grid=(bsz, N_HEADS // 2, s // tq),
        in_specs=[pl.BlockSpec((1, tq, 2 * LANE), lambda bi, hp, qi: (bi, qi, hp)), kv_spec, kv_spec],
        out_specs=pl.BlockSpec((1, tq, 2 * V_DIM), lambda bi, hp, qi: (bi, qi, hp)),
        out_shape=jax.ShapeDtypeStruct((bsz, s, N_HEADS * V_DIM), F32),
        scratch_shapes=[pltpu.VMEM((2, tq, tk), F32), pltpu.VMEM((2, tq, tk), F32), pltpu.VMEM((2, tq, tk), BF16),
                        pltpu.VMEM((2, tq, LANE), F32), pltpu.VMEM((2, tq, LANE), F32)],
        compiler_params=pltpu.CompilerParams(dimension_semantics=("arbitrary",) * 3, vmem_limit_bytes=VMEM_LIMIT),
        name="attn_causal",
    )(q, k, v)


def _attn_open_kernel(q_ref, kc_ref, vc_ref, kn_ref, vn_ref, o_ref):
    tq = q_ref.shape[1]
    outs = []
    for j in range(2):
        ls = slice(j * LANE, (j + 1) * LANE)
        q = q_ref[0, :, ls]
        m = jnp.full((tq, 1), NEG, F32)
        acc = jnp.zeros((tq, LANE), F32)
        m, acc = _attn_step(q, kc_ref[0, :, ls], vc_ref[0, :, ls], m, acc)
        m, acc = _attn_step(q, kn_ref[0, :, ls], vn_ref[0, :, ls], m, acc)
        outs.append(_attn_finish(acc))
    o_ref[0] = jnp.concatenate(outs, axis=1)


def _attn_open(q, kc, vc, kn, vn):
    bsz, s, _ = q.shape
    past = kc.shape[1]
    new = pl.BlockSpec((1, s, 2 * LANE), lambda bi, hp: (bi, 0, hp))
    old = pl.BlockSpec((1, past, 2 * LANE), lambda bi, hp: (bi, 0, hp))
    return pl.pallas_call(
        _attn_open_kernel,
        grid=(bsz, N_HEADS // 2),
        in_specs=[new, old, old, new, new],
        out_specs=pl.BlockSpec((1, s, 2 * V_DIM), lambda bi, hp: (bi, 0, hp)),
        out_shape=jax.ShapeDtypeStruct((bsz, s, N_HEADS * V_DIM), F32),
        compiler_params=pltpu.CompilerParams(dimension_semantics=("arbitrary",) * 2, vmem_limit_bytes=VMEM_LIMIT),
        name="attn_open",
    )(q, kc, vc, kn, vn)


def _post_kernel(x_ref, attn_ref, conv_ref, p_ref, ga_ref, wout_ref, gffn_ref, wup_ref, wdn_ref, gple_ref,
                 wgate_ref, wproj_ref, gfin_ref, y_ref):
    aw = attn_ref.shape[1]
    a = _rms(attn_ref[...], ga_ref[...]).astype(BF16)
    x = x_ref[...] + _dot(a, wout_ref[:aw, :]) + _dot(conv_ref[...], wout_ref[aw:, :])
    h = _rms(x, gffn_ref[...]).astype(BF16)
    ff = None
    for c in range(wup_ref.shape[1] // FF_CHUNK):
        cs = slice(c * FF_CHUNK, (c + 1) * FF_CHUNK)
        f = jnp.maximum(_dot(h, wup_ref[:, cs]), 0.0)
        part = _dot((f * f).astype(BF16), wdn_ref[cs, :])
        ff = part if ff is None else ff + part
    x = x + ff
    gate = jax.nn.sigmoid(_dot(_rms(x, gple_ref[...]).astype(BF16), wgate_ref[...]))
    x = x + gate * _dot(p_ref[...].astype(BF16), wproj_ref[...])
    y_ref[...] = _rms(x, gfin_ref[...])


def _post(x, attn, conv, p, ga, wout, gffn, wup, wdn, gple, wgate, wproj, gfin, tm):
    n, d = x.shape
    row = lambda w: pl.BlockSpec((tm, w), lambda i: (i, 0))
    return pl.pallas_call(
        _post_kernel,
        grid=(n // tm,),
        in_specs=[row(d), row(attn.shape[1]), row(conv.shape[1]), row(p.shape[1]), _full(ga.shape),
                  _resident(wout.shape), _full(gffn.shape), _resident(wup.shape), _resident(wdn.shape),
                  _full(gple.shape), _resident(wgate.shape), _resident(wproj.shape), _full(gfin.shape)],
        out_specs=row(d),
        out_shape=jax.ShapeDtypeStruct((n, d), F32),
        compiler_params=pltpu.CompilerParams(dimension_semantics=("arbitrary",), vmem_limit_bytes=VMEM_LIMIT),
        name="post",
    )(x, attn, conv, p, ga, wout, gffn, wup, wdn, gple, wgate, wproj, gfin)


def _swap_halves(w):
    half = w.shape[-1] // 2
    return jnp.concatenate([-w[..., half:], w[..., :half]], axis=-1)


def _prep_weights(w_in, w_uq, w_ukv):
    o_kr = Q_LORA + KV_LORA
    w_kr = w_in[:, o_kr:o_kr + QK_ROPE]
    w_kr_sw = _swap_halves(w_kr)
    win = jnp.concatenate([w_in[:, :o_kr], w_in[:, o_kr + QK_ROPE:], w_kr, w_kr_sw, w_kr, w_kr_sw], axis=1)

    pad = LANE - QK_DIM
    w3 = w_uq.reshape(Q_LORA, N_HEADS, QK_DIM)
    rope = w3[..., QK_NOPE:]
    wa = jnp.concatenate([w3, jnp.zeros((Q_LORA, N_HEADS, pad), F32)], axis=-1)
    wb = jnp.concatenate([jnp.zeros((Q_LORA, N_HEADS, QK_NOPE), F32), _swap_halves(rope),
                          jnp.zeros((Q_LORA, N_HEADS, pad), F32)], axis=-1)
    wuq = jnp.concatenate([wa.reshape(Q_LORA, HEAD_W), wb.reshape(Q_LORA, HEAD_W)], axis=1)

    kv3 = w_ukv.reshape(KV_LORA, N_HEADS, QK_NOPE + V_DIM)
    wk = jnp.concatenate([kv3[..., :QK_NOPE], jnp.zeros((KV_LORA, N_HEADS, LANE - QK_NOPE), F32)], axis=-1)
    wv = jnp.concatenate([kv3[..., QK_NOPE:], jnp.zeros((KV_LORA, N_HEADS, LANE - V_DIM), F32)], axis=-1)
    wkv = jnp.concatenate([wk.reshape(KV_LORA, HEAD_W), wv.reshape(KV_LORA, HEAD_W)], axis=1)
    return win.astype(BF16), wuq.astype(BF16), wkv.astype(BF16)


def _rope_tables(pos):
    half = QK_ROPE // 2
    inv = ROPE_BASE ** (-jnp.arange(half, dtype=F32) / half)
    ang = pos.astype(F32)[:, None] * inv[None, :]
    cos, sin = jnp.cos(ang), jnp.sin(ang)
    cos2 = jnp.concatenate([cos, cos], axis=1)
    sin2 = jnp.concatenate([sin, sin], axis=1)
    n = pos.shape[0]
    scale = QK_DIM ** -0.5 * 1.4426950408889634
    tqc = scale * jnp.concatenate([jnp.ones((n, QK_NOPE), F32), cos2, jnp.zeros((n, LANE - QK_DIM), F32)], axis=1)
    tqs = scale * jnp.concatenate([jnp.zeros((n, QK_NOPE), F32), sin2, jnp.zeros((n, LANE - QK_DIM), F32)], axis=1)
    tk = jnp.concatenate([cos2, sin2, cos2, sin2], axis=1)
    return tqc, tqs, tk


def _row_tile(n, want):
    t = min(n, want)
    while n % t:
        t //= 2
    return t


def _layer(x, p, pos, past, lw, fin_g):
    bsz, s, d = x.shape
    n = bsz * s
    assert s >= HALO, "the new conv state is taken from this step's rows only"
    tqc, tqs, tk = _rope_tables(pos)
    if past is None:
        conv_past = jnp.zeros((bsz, HALO, CONV_CH), F32)
    else:
        conv_past = jnp.pad(past[2], ((0, 0), (HALO - (CONV_K - 1), 0), (0, 0)))
    q, k, v, ckv, kr, conv, tail = _inproj(
        x.reshape(n, d), tqc, tqs, tk, conv_past, lw["gmix"], lw["win"], lw["gq"], lw["wuq"], lw["gkv"], lw["wkv"],
        lw["conv_w"], lw["conv_b"], lw["ln_g"], lw["ln_b"], lw["conv_og"], _row_tile(s, ROW_TILE))
    q3, k3, v3 = (t.reshape(bsz, s, HEAD_W) for t in (q, k, v))
    if past is None:
        attn = _attn_causal(q3, k3, v3, _row_tile(s, ATTN_Q_ROWS), _row_tile(s, ATTN_Q_ROWS // 2))
    else:
        plen = past[0].shape[1]
        krp = jnp.pad(past[1].reshape(bsz * plen, QK_ROPE), ((0, 0), (QK_NOPE, LANE - QK_DIM)))
        kc, vc = _kvexpand(past[0].reshape(bsz * plen, KV_LORA), krp, lw["wkv"], _row_tile(bsz * plen, ROW_TILE))
        attn = _attn_open(q3, kc.reshape(bsz, plen, HEAD_W), vc.reshape(bsz, plen, HEAD_W), k3, v3)
    y = _post(x.reshape(n, d), attn.reshape(n, -1), conv, p.reshape(n, -1), lw["ga"], lw["wout"], lw["gffn"],
              lw["wup"], lw["wdn"], lw["gple"], lw["wgate"], lw["wproj"], fin_g, _row_tile(n, ROW_TILE))
    conv_new = tail[:, HALO - (CONV_K - 1):]
    return y.reshape(bsz, s, d), ckv.reshape(bsz, s, KV_LORA), kr.reshape(bsz, s, QK_ROPE), conv_new


def kernel(x_prompt, x_sample, cache_ckv, cache_krope, state_conv, p_prompt, p_sample, norm_mix_g, w_in, q_norm_g, w_uq, kv_norm_g, w_ukv, conv_w, conv_b, conv_ln_g, conv_ln_b, attn_out_g, conv_out_g, w_out, norm_ffn_g, w_ff_up, w_ff_down, norm_ple_g, w_ple_gate, w_ple_proj, norm_final_g):
    depth = w_in.shape[0]
    assert depth == 1, "the final RMSNorm is fused into the (single) layer's last kernel"
    past_len = cache_ckv.shape[2]
    pos_prompt = jnp.arange(x_prompt.shape[1])
    pos_sample = past_len + jnp.arange(x_sample.shape[1])
    row = lambda g: g.reshape(1, -1)
    outs_p, outs_s = [], []
    hp, hs = x_prompt, x_sample
    for i in range(depth):
        win, wuq, wkv = _prep_weights(w_in[i], w_uq[i], w_ukv[i])
        lw = dict(
            gmix=row(norm_mix_g[i]), win=win, gq=row(q_norm_g[i]), wuq=wuq, gkv=row(kv_norm_g[i]), wkv=wkv,
            conv_w=jnp.repeat(conv_w[i], SUBLANE, axis=0), conv_b=row(conv_b[i]),
            ln_g=row(conv_ln_g[i]), ln_b=row(conv_ln_b[i]), conv_og=row(conv_out_g[i]),
            ga=row(attn_out_g[i]), wout=w_out[i].astype(BF16), gffn=row(norm_ffn_g[i]),
            wup=w_ff_up[i].astype(BF16), wdn=w_ff_down[i].astype(BF16), gple=row(norm_ple_g[i]),
            wgate=w_ple_gate[i].astype(BF16), wproj=w_ple_proj[i].astype(BF16))
        fin = row(norm_final_g)
        hp, a, b, c = _layer(hp, p_prompt[i], pos_prompt, None, lw, fin)
        outs_p.append((a, b, c))
        hs, a, b, c = _layer(hs, p_sample[i], pos_sample, (cache_ckv[i], cache_krope[i], state_conv[i]), lw, fin)
        outs_s.append((a, b, c))
    stack = lambda outs, j: jnp.stack([o[j] for o in outs])
    return (hp, hs, stack(outs_p, 0), stack(outs_p, 1), stack(outs_p, 2),
            stack(outs_s, 0), stack(outs_s, 1), stack(outs_s, 2))
```

```python
import functools

import jax
import jax.numpy as jnp
from jax import lax
from jax.experimental import pallas as pl
from jax.experimental.pallas import tpu as pltpu

F32 = jnp.float32
BF16 = jnp.bfloat16

CHUNK = 64
N_HEADS = 8
QK_NOPE = 64
QK_ROPE = 32
V_DIM = 64
Q_LORA = 384
KV_LORA = 256
ROPE_BASE = 10000.0
CONV_CH = 512
CONV_K = 31
QK_DIM = QK_NOPE + QK_ROPE
EPS = 1e-6
LN_EPS = 1e-5
NEG = -1e30

LANE = 128
SUBLANE = 8
HEAD_W = N_HEADS * LANE
HALO = 32
CONV_ROWS = 32
SOFTMAX_ROWS = 64
FF_CHUNK = 1024
ROW_TILE = 512
ATTN_Q_ROWS = 1024
VMEM_LIMIT = 56 * 1024 * 1024


def _rms(x, g):
    return x * lax.rsqrt(jnp.mean(x * x, axis=-1, keepdims=True) + EPS) * g


def _dot(a, b):
    return jnp.dot(a, b, preferred_element_type=F32)


def _full(shape):
    return pl.BlockSpec(shape, lambda *_: (0,) * len(shape))


def _resident(shape):
    return pl.BlockSpec(shape, lambda *_: (0,) * len(shape), pipeline_mode=pl.Buffered(1))


def _conv_shifts(ext_ref, sh_ref, tc):
    for d in range(SUBLANE):
        span = tc + HALO - (SUBLANE if d else 0)
        sh_ref[d, 0:span, :] = ext_ref[d:d + span, :]


def _conv_rows(sh_ref, w_ref, b_ref, lng_ref, lnb_ref, og_ref, o_ref, row_blocks):
    lead = HALO - (CONV_K - 1)
    for r in row_blocks:
        row0 = r * CONV_ROWS
        acc = jnp.broadcast_to(b_ref[...], (CONV_ROWS, CONV_CH))
        for j in range(CONV_K):
            whole, d = divmod(lead + j, SUBLANE)
            wj = jnp.tile(w_ref[j * SUBLANE:(j + 1) * SUBLANE, :], (CONV_ROWS // SUBLANE, 1))
            acc = acc + wj * sh_ref[d, pl.ds(row0 + whole * SUBLANE, CONV_ROWS), :]
        mu = jnp.mean(acc, axis=-1, keepdims=True)
        dlt = acc - mu
        var = jnp.mean(dlt * dlt, axis=-1, keepdims=True)
        y = dlt * lax.rsqrt(var + LN_EPS) * lng_ref[...] + lnb_ref[...]
        y = y * jax.nn.sigmoid(y)
        o_ref[pl.ds(row0, CONV_ROWS), :] = _rms(y, og_ref[...]).astype(o_ref.dtype)


def _inproj_kernel(x_ref, tqc_ref, tqs_ref, tk_ref, past_ref, gmix_ref, win_ref, gq_ref, wuq_ref, gkv_ref, wkv_ref,
                   cw_ref, cb_ref, lng_ref, lnb_ref, og_ref,
                   q_ref, k_ref, v_ref, ckv_ref, kr_ref, conv_ref, tail_ref, ext_ref, sh_ref, *, tiles_per_seq):
    tm = x_ref.shape[0]
    first = pl.program_id(0) % tiles_per_seq == 0

    @pl.when(first)
    def _():
        ext_ref[0:HALO, :] = past_ref[0]

    @pl.when(jnp.logical_not(first))
    def _():
        ext_ref[0:HALO, :] = ext_ref[tm:tm + HALO, :]

    h = _rms(x_ref[...], gmix_ref[...]).astype(BF16)
    z = _dot(h, win_ref[...])
    o_kv = Q_LORA
    o_ga = o_kv + KV_LORA
    o_gb = o_ga + CONV_CH
    o_kr = o_gb + CONV_CH

    cq = _rms(z[:, :o_kv], gq_ref[...]).astype(BF16)
    qq = _dot(cq, wuq_ref[...])
    tqc = tqc_ref[...]
    tqs = tqs_ref[...]
    for hd in range(N_HEADS):
        a = qq[:, hd * LANE:(hd + 1) * LANE]
        b = qq[:, HEAD_W + hd * LANE:HEAD_W + (hd + 1) * LANE]
        q_ref[:, hd * LANE:(hd + 1) * LANE] = (a * tqc + b * tqs).astype(BF16)

    ckv = _rms(z[:, o_kv:o_ga], gkv_ref[...])
    ckv_ref[...] = ckv
    kv = _dot(ckv.astype(BF16), wkv_ref[...])

    t = z[:, o_kr:o_kr + LANE] * tk_ref[...]
    krot = t + pltpu.roll(t, QK_ROPE, 1)
    kr_ref[...] = krot[:, :QK_ROPE]
    lane = lax.broadcasted_iota(jnp.int32, krot.shape, 1)
    kr_at_rope = jnp.where((lane >= QK_NOPE) & (lane < QK_DIM), krot, 0.0)
    one_at_sum = jnp.where(lane == V_DIM, 1.0, 0.0)
    for hd in range(N_HEADS):
        k_ref[:, hd * LANE:(hd + 1) * LANE] = (kv[:, hd * LANE:(hd + 1) * LANE] + kr_at_rope).astype(BF16)
        v_ref[:, hd * LANE:(hd + 1) * LANE] = (
            kv[:, HEAD_W + hd * LANE:HEAD_W + (hd + 1) * LANE] + one_at_sum).astype(BF16)

    u = z[:, o_ga:o_gb] * jax.nn.sigmoid(z[:, o_gb:o_kr])
    ext_ref[HALO:, :] = u
    tail_ref[0] = u[tm - HALO:, :]
    _conv_shifts(ext_ref, sh_ref, tm)
    _conv_rows(sh_ref, cw_ref, cb_ref, lng_ref, lnb_ref, og_ref, conv_ref, range(tm // CONV_ROWS))


def _inproj(x, tqc, tqs, tk, past, gmix, win, gq, wuq, gkv, wkv, cw, cb, lng, lnb, og, tm):
    n, d = x.shape
    nt = tqc.shape[0] // tm
    row = lambda w: pl.BlockSpec((tm, w), lambda i: (i, 0))
    tab = pl.BlockSpec((tm, LANE), lambda i: (i % nt, 0))
    seq = pl.BlockSpec((1, HALO, CONV_CH), lambda i: (i // nt, 0, 0))
    return pl.pallas_call(
        functools.partial(_inproj_kernel, tiles_per_seq=nt),
        grid=(n // tm,),
        in_specs=[row(d), tab, tab, tab, seq, _full(gmix.shape), _resident(win.shape), _full(gq.shape),
                  _resident(wuq.shape), _full(gkv.shape), _resident(wkv.shape), _full(cw.shape), _full(cb.shape),
                  _full(lng.shape), _full(lnb.shape), _full(og.shape)],
        out_specs=[row(HEAD_W), row(HEAD_W), row(HEAD_W), row(KV_LORA), row(QK_ROPE), row(CONV_CH), seq],
        out_shape=[jax.ShapeDtypeStruct((n, HEAD_W), BF16), jax.ShapeDtypeStruct((n, HEAD_W), BF16),
                   jax.ShapeDtypeStruct((n, HEAD_W), BF16), jax.ShapeDtypeStruct((n, KV_LORA), F32),
                   jax.ShapeDtypeStruct((n, QK_ROPE), F32), jax.ShapeDtypeStruct((n, CONV_CH), BF16),
                   jax.ShapeDtypeStruct(past.shape, F32)],
        scratch_shapes=[pltpu.VMEM((tm + HALO, CONV_CH), F32), pltpu.VMEM((SUBLANE, tm + HALO, CONV_CH), F32)],
        compiler_params=pltpu.CompilerParams(dimension_semantics=("arbitrary",), vmem_limit_bytes=VMEM_LIMIT),
        name="inproj",
    )(x, tqc, tqs, tk, past, gmix, win, gq, wuq, gkv, wkv, cw, cb, lng, lnb, og)


def _kvexpand_kernel(ckv_ref, krp_ref, wkv_ref, k_ref, v_ref):
    kv = _dot(ckv_ref[...].astype(BF16), wkv_ref[...])
    krp = krp_ref[...]
    lane = lax.broadcasted_iota(jnp.int32, krp.shape, 1)
    one_at_sum = jnp.where(lane == V_DIM, 1.0, 0.0)
    for hd in range(N_HEADS):
        k_ref[:, hd * LANE:(hd + 1) * LANE] = (kv[:, hd * LANE:(hd + 1) * LANE] + krp).astype(BF16)
        v_ref[:, hd * LANE:(hd + 1) * LANE] = (
            kv[:, HEAD_W + hd * LANE:HEAD_W + (hd + 1) * LANE] + one_at_sum).astype(BF16)


def _kvexpand(ckv, krp, wkv, tm):
    n = ckv.shape[0]
    row = lambda w: pl.BlockSpec((tm, w), lambda i: (i, 0))
    return pl.pallas_call(
        _kvexpand_kernel,
        grid=(n // tm,),
        in_specs=[row(KV_LORA), row(LANE), _resident(wkv.shape)],
        out_specs=[row(HEAD_W), row(HEAD_W)],
        out_shape=[jax.ShapeDtypeStruct((n, HEAD_W), BF16), jax.ShapeDtypeStruct((n, HEAD_W), BF16)],
        compiler_params=pltpu.CompilerParams(dimension_semantics=("arbitrary",), vmem_limit_bytes=VMEM_LIMIT),
        name="kvexpand",
    )(ckv, krp, wkv)


def _scores(q, kblk):
    return lax.dot_general(q, kblk, (((1,), (1,)), ((), ())), preferred_element_type=F32)


def _softmax_pv(s, vblk, m, acc, mask=None):
    if mask is not None:
        s = jnp.where(mask, s, NEG)
    m_new = jnp.maximum(m, jnp.max(s, axis=-1, keepdims=True))
    p = jnp.exp2((s - m_new).astype(BF16))
    acc = jnp.exp2(m - m_new) * acc + _dot(p, vblk)
    return m_new, acc


def _attn_step(q, kblk, vblk, m, acc, mask=None):
    return _softmax_pv(_scores(q, kblk), vblk, m, acc, mask)


def _attn_finish(acc):
    return (acc / acc[:, V_DIM:V_DIM + 1])[:, :V_DIM]


def _attn_causal_kernel(q_ref, k_ref, v_ref, o_ref, a0_ref, a1_ref, b0_ref, b1_ref, p_ref, m_ref, acc_ref, *, tq, tk):
    qi = pl.program_id(2)
    lanes = [slice(j * LANE, (j + 1) * LANE) for j in range(2)]
    bufs = ((a0_ref, a1_ref), (b0_ref, b1_ref))
    halves = tq // tk

    def scores(g, into):
        for h, s_ref in enumerate(bufs[into]):
            start = pl.multiple_of(g * tq + h * tk, tk)
            for j, ls in enumerate(lanes):
                s_ref[j] = _scores(q_ref[0, :, ls], k_ref[0, pl.ds(start, tk), ls])

    def update(g, frm, rows=slice(0, tq), blocks=halves, masks=None):
        start = pl.multiple_of(g * tq, tq)
        for j, ls in enumerate(lanes):
            for r0 in range(rows.start, rows.stop, SOFTMAX_ROWS):
                rc = slice(r0, r0 + SOFTMAX_ROWS)
                s = [s_ref[j, rc, :] for s_ref in bufs[frm][:blocks]]
                if masks is not None:
                    local = slice(r0 - rows.start, r0 - rows.start + SOFTMAX_ROWS)
                    s = [x if mk is None else jnp.where(mk[local], x, NEG) for x, mk in zip(s, masks)]
                m_old = m_ref[j, rc, :]
                m_new = functools.reduce(jnp.maximum, [m_old] + [jnp.max(x, axis=-1, keepdims=True) for x in s])
                m_ref[j, rc, :] = m_new
                acc_ref[j, rc, :] = acc_ref[j, rc, :] * jnp.exp2(m_old - m_new)
                m_wide = jnp.concatenate([m_new] * (tk // LANE), axis=1)
                for h in range(blocks):
                    p_ref[j, rc, h * tk:(h + 1) * tk] = jnp.exp2((s[h] - m_wide).astype(BF16))
            acc_ref[j, rows, :] += _dot(p_ref[j, rows, :blocks * tk], v_ref[0, pl.ds(start, blocks * tk), ls])

    def step(g, frm):
        scores(g + 1, 1 - frm)
        update(g, frm)

    def two_steps(i, carry):
        step(2 * i, 0)
        step(2 * i + 1, 1)
        return carry

    def own_group(frm):
        chunk_of = lambda axis: lax.broadcasted_iota(jnp.int32, (tk, tk), axis) // CHUNK
        diag = chunk_of(1) <= chunk_of(0)
        for h in range(halves):
            update(qi, frm, slice(h * tk, (h + 1) * tk), h + 1, [None] * h + [diag])
        o_ref[0] = jnp.concatenate([_attn_finish(acc_ref[j]) for j in range(2)], axis=1)

    m_ref[...] = jnp.full(m_ref.shape, NEG, F32)
    acc_ref[...] = jnp.zeros(acc_ref.shape, F32)
    scores(0, 0)
    lax.fori_loop(0, qi // 2, two_steps, 0)
    odd = qi % 2 == 1

    @pl.when(odd)
    def _():
        step(qi - 1, 0)
        own_group(1)

    @pl.when(jnp.logical_not(odd))
    def _():
        own_group(0)


def _attn_causal(q, k, v, tq, tk):
    assert tq == 2 * tk
    bsz, s, _ = q.shape
    kv_spec = pl.BlockSpec((1, s, 2 * LANE), lambda bi, hp, qi: (bi, 0, hp))
    return pl.pallas_call(
        functools.partial(_attn_causal_kernel, tq=tq, tk=tk),
        grid=(bsz, N_HEADS // 2, s // tq),
        in_specs=[pl.BlockSpec((1, tq, 2 * LANE), lambda bi, hp, qi: (bi, qi, hp)), kv_spec, kv_spec],
        out_specs=pl.BlockSpec((1, tq, 2 * V_DIM), lambda bi, hp, qi: (bi, qi, hp)),
        out_shape=jax.ShapeDtypeStruct((bsz, s, N_HEADS * V_DIM), F32),
        scratch_shapes=([pltpu.VMEM((2, tq, tk), F32)] * 4 + [pltpu.VMEM((2, tq, tq), BF16)]
                        + [pltpu.VMEM((2, tq, LANE), F32)] * 2),
        compiler_params=pltpu.CompilerParams(dimension_semantics=("arbitrary",) * 3, vmem_limit_bytes=VMEM_LIMIT),
        name="attn_causal",
    )(q, k, v)


def _attn_open_kernel(q_ref, kc_ref, vc_ref, kn_ref, vn_ref, o_ref):
    tq = q_ref.shape[1]
    outs = []
    for j in range(2):
        ls = slice(j * LANE, (j + 1) * LANE)
        q = q_ref[0, :, ls]
        m = jnp.full((tq, 1), NEG, F32)
        acc = jnp.zeros((tq, LANE), F32)
        m, acc = _attn_step(q, kc_ref[0, :, ls], vc_ref[0, :, ls], m, acc)
        m, acc = _attn_step(q, kn_ref[0, :, ls], vn_ref[0, :, ls], m, acc)
        outs.append(_attn_finish(acc))
    o_ref[0] = jnp.concatenate(outs, axis=1)


def _attn_open(q, kc, vc, kn, vn):
    bsz, s, _ = q.shape
    past = kc.shape[1]
    new = pl.BlockSpec((1, s, 2 * LANE), lambda bi, hp: (bi, 0, hp))
    old = pl.BlockSpec((1, past, 2 * LANE), lambda bi, hp: (bi, 0, hp))
    return pl.pallas_call(
        _attn_open_kernel,
        grid=(bsz, N_HEADS // 2),
        in_specs=[new, old, old, new, new],
        out_specs=pl.BlockSpec((1, s, 2 * V_DIM), lambda bi, hp: (bi, 0, hp)),
        out_shape=jax.ShapeDtypeStruct((bsz, s, N_HEADS * V_DIM), F32),
        compiler_params=pltpu.CompilerParams(dimension_semantics=("arbitrary",) * 2, vmem_limit_bytes=VMEM_LIMIT),
        name="attn_open",
    )(q, kc, vc, kn, vn)


def _post_kernel(x_ref, attn_ref, conv_ref, p_ref, ga_ref, wout_ref, gffn_ref, wup_ref, wdn_ref, gple_ref,
                 wgate_ref, wproj_ref, gfin_ref, y_ref):
    aw = attn_ref.shape[1]
    a = _rms(attn_ref[...], ga_ref[...]).astype(BF16)
    x = x_ref[...] + _dot(a, wout_ref[:aw, :]) + _dot(conv_ref[...], wout_ref[aw:, :])
    h = _rms(x, gffn_ref[...]).astype(BF16)
    ff = None
    for c in range(wup_ref.shape[1] // FF_CHUNK):
        cs = slice(c * FF_CHUNK, (c + 1) * FF_CHUNK)
        f = jnp.maximum(_dot(h, wup_ref[:, cs]), 0.0)
        part = _dot((f * f).astype(BF16), wdn_ref[cs, :])
        ff = part if ff is None else ff + part
    x = x + ff
    gate = jax.nn.sigmoid(_dot(_rms(x, gple_ref[...]).astype(BF16), wgate_ref[...]))
    x = x + gate * _dot(p_ref[...].astype(BF16), wproj_ref[...])
    y_ref[...] = _rms(x, gfin_ref[...])


def _post(x, attn, conv, p, ga, wout, gffn, wup, wdn, gple, wgate, wproj, gfin, tm):
    n, d = x.shape
    row = lambda w: pl.BlockSpec((tm, w), lambda i: (i, 0))
    return pl.pallas_call(
        _post_kernel,
        grid=(n // tm,),
        in_specs=[row(d), row(attn.shape[1]), row(conv.shape[1]), row(p.shape[1]), _full(ga.shape),
                  _resident(wout.shape), _full(gffn.shape), _resident(wup.shape), _resident(wdn.shape),
                  _full(gple.shape), _resident(wgate.shape), _resident(wproj.shape), _full(gfin.shape)],
        out_specs=row(d),
        out_shape=jax.ShapeDtypeStruct((n, d), F32),
        compiler_params=pltpu.CompilerParams(dimension_semantics=("arbitrary",), vmem_limit_bytes=VMEM_LIMIT),
        name="post",
    )(x, attn, conv, p, ga, wout, gffn, wup, wdn, gple, wgate, wproj, gfin)


def _swap_halves(w):
    half = w.shape[-1] // 2
    return jnp.concatenate([-w[..., half:], w[..., :half]], axis=-1)


def _prep_weights(w_in, w_uq, w_ukv):
    o_kr = Q_LORA + KV_LORA
    w_kr = w_in[:, o_kr:o_kr + QK_ROPE]
    w_kr_sw = _swap_halves(w_kr)
    win = jnp.concatenate([w_in[:, :o_kr], w_in[:, o_kr + QK_ROPE:], w_kr, w_kr_sw, w_kr, w_kr_sw], axis=1)

    pad = LANE - QK_DIM
    w3 = w_uq.reshape(Q_LORA, N_HEADS, QK_DIM)
    rope = w3[..., QK_NOPE:]
    wa = jnp.concatenate([w3, jnp.zeros((Q_LORA, N_HEADS, pad), F32)], axis=-1)
    wb = jnp.concatenate([jnp.zeros((Q_LORA, N_HEADS, QK_NOPE), F32), _swap_halves(rope),
                          jnp.zeros((Q_LORA, N_HEADS, pad), F32)], axis=-1)
    wuq = jnp.concatenate([wa.reshape(Q_LORA, HEAD_W), wb.reshape(Q_LORA, HEAD_W)], axis=1)

    kv3 = w_ukv.reshape(KV_LORA, N_HEADS, QK_NOPE + V_DIM)
    wk = jnp.concatenate([kv3[..., :QK_NOPE], jnp.zeros((KV_LORA, N_HEADS, LANE - QK_NOPE), F32)], axis=-1)
    wv = jnp.concatenate([kv3[..., QK_NOPE:], jnp.zeros((KV_LORA, N_HEADS, LANE - V_DIM), F32)], axis=-1)
    wkv = jnp.concatenate([wk.reshape(KV_LORA, HEAD_W), wv.reshape(KV_LORA, HEAD_W)], axis=1)
    return win.astype(BF16), wuq.astype(BF16), wkv.astype(BF16)


def _rope_tables(pos):
    half = QK_ROPE // 2
    inv = ROPE_BASE ** (-jnp.arange(half, dtype=F32) / half)
    ang = pos.astype(F32)[:, None] * inv[None, :]
    cos, sin = jnp.cos(ang), jnp.sin(ang)
    cos2 = jnp.concatenate([cos, cos], axis=1)
    sin2 = jnp.concatenate([sin, sin], axis=1)
    n = pos.shape[0]
    scale = QK_DIM ** -0.5 * 1.4426950408889634
    tqc = scale * jnp.concatenate([jnp.ones((n, QK_NOPE), F32), cos2, jnp.zeros((n, LANE - QK_DIM), F32)], axis=1)
    tqs = scale * jnp.concatenate([jnp.zeros((n, QK_NOPE), F32), sin2, jnp.zeros((n, LANE - QK_DIM), F32)], axis=1)
    tk = jnp.concatenate([cos2, sin2, cos2, sin2], axis=1)
    return tqc, tqs, tk


def _row_tile(n, want):
    t = min(n, want)
    while n % t:
        t //= 2
    return t


def _layer(x, p, pos, past, lw, fin_g):
    bsz, s, d = x.shape
    n = bsz * s
    assert s >= HALO, "the new conv state is taken from this step's rows only"
    tqc, tqs, tk = _rope_tables(pos)
    if past is None:
        conv_past = jnp.zeros((bsz, HALO, CONV_CH), F32)
    else:
        conv_past = jnp.pad(past[2], ((0, 0), (HALO - (CONV_K - 1), 0), (0, 0)))
    q, k, v, ckv, kr, conv, tail = _inproj(
        x.reshape(n, d), tqc, tqs, tk, conv_past, lw["gmix"], lw["win"], lw["gq"], lw["wuq"], lw["gkv"], lw["wkv"],
        lw["conv_w"], lw["conv_b"], lw["ln_g"], lw["ln_b"], lw["conv_og"], _row_tile(s, ROW_TILE))
    q3, k3, v3 = (t.reshape(bsz, s, HEAD_W) for t in (q, k, v))
    if past is None:
        attn = _attn_causal(q3, k3, v3, _row_tile(s, ATTN_Q_ROWS), _row_tile(s, ATTN_Q_ROWS // 2))
    else:
        plen = past[0].shape[1]
        krp = jnp.pad(past[1].reshape(bsz * plen, QK_ROPE), ((0, 0), (QK_NOPE, LANE - QK_DIM)))
        kc, vc = _kvexpand(past[0].reshape(bsz * plen, KV_LORA), krp, lw["wkv"], _row_tile(bsz * plen, ROW_TILE))
        attn = _attn_open(q3, kc.reshape(bsz, plen, HEAD_W), vc.reshape(bsz, plen, HEAD_W), k3, v3)
    y = _post(x.reshape(n, d), attn.reshape(n, -1), conv, p.reshape(n, -1), lw["ga"], lw["wout"], lw["gffn"],
              lw["wup"], lw["wdn"], lw["gple"], lw["wgate"], lw["wproj"], fin_g, _row_tile(n, ROW_TILE))
    conv_new = tail[:, HALO - (CONV_K - 1):]
    return y.reshape(bsz, s, d), ckv.reshape(bsz, s, KV_LORA), kr.reshape(bsz, s, QK_ROPE), conv_new


def kernel(x_prompt, x_sample, cache_ckv, cache_krope, state_conv, p_prompt, p_sample, norm_mix_g, w_in, q_norm_g, w_uq, kv_norm_g, w_ukv, conv_w, conv_b, conv_ln_g, conv_ln_b, attn_out_g, conv_out_g, w_out, norm_ffn_g, w_ff_up, w_ff_down, norm_ple_g, w_ple_gate, w_ple_proj, norm_final_g):
    depth = w_in.shape[0]
    assert depth == 1, "the final RMSNorm is fused into the (single) layer's last kernel"
    past_len = cache_ckv.shape[2]
    pos_prompt = jnp.arange(x_prompt.shape[1])
    pos_sample = past_len + jnp.arange(x_sample.shape[1])
    row = lambda g: g.reshape(1, -1)
    outs_p, outs_s = [], []
    hp, hs = x_prompt, x_sample
    for i in range(depth):
        win, wuq, wkv = _prep_weights(w_in[i], w_uq[i], w_ukv[i])
        lw = dict(
            gmix=row(norm_mix_g[i]), win=win, gq=row(q_norm_g[i]), wuq=wuq, gkv=row(kv_norm_g[i]), wkv=wkv,
            conv_w=jnp.repeat(conv_w[i], SUBLANE, axis=0), conv_b=row(conv_b[i]),
            ln_g=row(conv_ln_g[i]), ln_b=row(conv_ln_b[i]), conv_og=row(conv_out_g[i]),
            ga=row(attn_out_g[i]), wout=w_out[i].astype(BF16), gffn=row(norm_ffn_g[i]),
            wup=w_ff_up[i].astype(BF16), wdn=w_ff_down[i].astype(BF16), gple=row(norm_ple_g[i]),
            wgate=w_ple_gate[i].astype(BF16), wproj=w_ple_proj[i].astype(BF16))
        fin = row(norm_final_g)
        hp, a, b, c = _layer(hp, p_prompt[i], pos_prompt, None, lw, fin)
        outs_p.append((a, b, c))
        hs, a, b, c = _layer(hs, p_sample[i], pos_sample, (cache_ckv[i], cache_krope[i], state_conv[i]), lw, fin)
        outs_s.append((a, b, c))
    stack = lambda outs, j: jnp.stack([o[j] for o in outs])
    return (hp, hs, stack(outs_p, 0), stack(outs_p, 1), stack(outs_p, 2),
            stack(outs_s, 0), stack(outs_s, 1), stack(outs_s, 2))
```

```python
import functools

import jax
import jax.numpy as jnp
from jax import lax
from jax.experimental import pallas as pl
from jax.experimental.pallas import tpu as pltpu

F32 = jnp.float32
BF16 = jnp.bfloat16

CHUNK = 64
N_HEADS = 8
QK_NOPE = 64
QK_ROPE = 32
V_DIM = 64
Q_LORA = 384
KV_LORA = 256
ROPE_BASE = 10000.0
CONV_CH = 512
CONV_K = 31
QK_DIM = QK_NOPE + QK_ROPE
EPS = 1e-6
LN_EPS = 1e-5
NEG = -1e30

LANE = 128
SUBLANE = 8
HEAD_W = N_HEADS * LANE
HALO = 32
CONV_ROWS = 32
SOFTMAX_ROWS = 64
FF_CHUNK = 1024
ROW_TILE = 512
ATTN_Q_ROWS = 1024
VMEM_LIMIT = 56 * 1024 * 1024


def _rms(x, g):
    return x * lax.rsqrt(jnp.mean(x * x, axis=-1, keepdims=True) + EPS) * g


def _dot(a, b):
    return jnp.dot(a, b, preferred_element_type=F32)


def _full(shape):
    return pl.BlockSpec(shape, lambda *_: (0,) * len(shape))


def _resident(shape):
    return pl.BlockSpec(shape, lambda *_: (0,) * len(shape), pipeline_mode=pl.Buffered(1))


def _conv_shifts(ext_ref, sh_ref, tc):
    for d in range(SUBLANE):
        span = tc + HALO - (SUBLANE if d else 0)
        sh_ref[d, 0:span, :] = ext_ref[d:d + span, :]


def _conv_rows(sh_ref, w_ref, b_ref, lng_ref, lnb_ref, og_ref, o_ref, row_blocks):
    lead = HALO - (CONV_K - 1)
    for r in row_blocks:
        row0 = r * CONV_ROWS
        acc = jnp.broadcast_to(b_ref[...], (CONV_ROWS, CONV_CH))
        for j in range(CONV_K):
            whole, d = divmod(lead + j, SUBLANE)
            wj = jnp.tile(w_ref[j * SUBLANE:(j + 1) * SUBLANE, :], (CONV_ROWS // SUBLANE, 1))
            acc = acc + wj * sh_ref[d, pl.ds(row0 + whole * SUBLANE, CONV_ROWS), :]
        mu = jnp.mean(acc, axis=-1, keepdims=True)
        dlt = acc - mu
        var = jnp.mean(dlt * dlt, axis=-1, keepdims=True)
        y = dlt * lax.rsqrt(var + LN_EPS) * lng_ref[...] + lnb_ref[...]
        y = y * jax.nn.sigmoid(y)
        o_ref[pl.ds(row0, CONV_ROWS), :] = _rms(y, og_ref[...]).astype(o_ref.dtype)


def _inproj_kernel(x_ref, tqc_ref, tqs_ref, tk_ref, past_ref, gmix_ref, win_ref, gq_ref, wuq_ref, gkv_ref, wkv_ref,
                   cw_ref, cb_ref, lng_ref, lnb_ref, og_ref,
                   q_ref, k_ref, v_ref, ckv_ref, kr_ref, conv_ref, tail_ref, ext_ref, sh_ref, *, tiles_per_seq):
    tm = x_ref.shape[0]
    first = pl.program_id(0) % tiles_per_seq == 0

    @pl.when(first)
    def _():
        ext_ref[0:HALO, :] = past_ref[0]

    @pl.when(jnp.logical_not(first))
    def _():
        ext_ref[0:HALO, :] = ext_ref[tm:tm + HALO, :]

    h = _rms(x_ref[...], gmix_ref[...]).astype(BF16)
    z = _dot(h, win_ref[...])
    o_kv = Q_LORA
    o_ga = o_kv + KV_LORA
    o_gb = o_ga + CONV_CH
    o_kr = o_gb + CONV_CH

    cq = _rms(z[:, :o_kv], gq_ref[...]).astype(BF16)
    qq = _dot(cq, wuq_ref[...])
    tqc = tqc_ref[...]
    tqs = tqs_ref[...]
    for hd in range(N_HEADS):
        a = qq[:, hd * LANE:(hd + 1) * LANE]
        b = qq[:, HEAD_W + hd * LANE:HEAD_W + (hd + 1) * LANE]
        q_ref[:, hd * LANE:(hd + 1) * LANE] = (a * tqc + b * tqs).astype(BF16)

    ckv = _rms(z[:, o_kv:o_ga], gkv_ref[...])
    ckv_ref[...] = ckv
    kv = _dot(ckv.astype(BF16), wkv_ref[...])

    t = z[:, o_kr:o_kr + LANE] * tk_ref[...]
    krot = t + pltpu.roll(t, QK_ROPE, 1)
    kr_ref[...] = krot[:, :QK_ROPE]
    lane = lax.broadcasted_iota(jnp.int32, krot.shape, 1)
    kr_at_rope = jnp.where((lane >= QK_NOPE) & (lane < QK_DIM), krot, 0.0)
    one_at_sum = jnp.where(lane == V_DIM, 1.0, 0.0)
    for hd in range(N_HEADS):
        k_ref[:, hd * LANE:(hd + 1) * LANE] = (kv[:, hd * LANE:(hd + 1) * LANE] + kr_at_rope).astype(BF16)
        v_ref[:, hd * LANE:(hd + 1) * LANE] = (
            kv[:, HEAD_W + hd * LANE:HEAD_W + (hd + 1) * LANE] + one_at_sum).astype(BF16)

    u = z[:, o_ga:o_gb] * jax.nn.sigmoid(z[:, o_gb:o_kr])
    ext_ref[HALO:, :] = u
    tail_ref[0] = u[tm - HALO:, :]
    _conv_shifts(ext_ref, sh_ref, tm)
    _conv_rows(sh_ref, cw_ref, cb_ref, lng_ref, lnb_ref, og_ref, conv_ref, range(tm // CONV_ROWS))


def _inproj(x, tqc, tqs, tk, past, gmix, win, gq, wuq, gkv, wkv, cw, cb, lng, lnb, og, tm):
    n, d = x.shape
    nt = tqc.shape[0] // tm
    row = lambda w: pl.BlockSpec((tm, w), lambda i: (i, 0))
    tab = pl.BlockSpec((tm, LANE), lambda i: (i % nt, 0))
    seq = pl.BlockSpec((1, HALO, CONV_CH), lambda i: (i // nt, 0, 0))
    return pl.pallas_call(
        functools.partial(_inproj_kernel, tiles_per_seq=nt),
        grid=(n // tm,),
        in_specs=[row(d), tab, tab, tab, seq, _full(gmix.shape), _resident(win.shape), _full(gq.shape),
                  _resident(wuq.shape), _full(gkv.shape), _resident(wkv.shape), _full(cw.shape), _full(cb.shape),
                  _full(lng.shape), _full(lnb.shape), _full(og.shape)],
        out_specs=[row(HEAD_W), row(HEAD_W), row(HEAD_W), row(KV_LORA), row(QK_ROPE), row(CONV_CH), seq],
        out_shape=[jax.ShapeDtypeStruct((n, HEAD_W), BF16), jax.ShapeDtypeStruct((n, HEAD_W), BF16),
                   jax.ShapeDtypeStruct((n, HEAD_W), BF16), jax.ShapeDtypeStruct((n, KV_LORA), F32),
                   jax.ShapeDtypeStruct((n, QK_ROPE), F32), jax.ShapeDtypeStruct((n, CONV_CH), BF16),
                   jax.ShapeDtypeStruct(past.shape, F32)],
        scratch_shapes=[pltpu.VMEM((tm + HALO, CONV_CH), F32), pltpu.VMEM((SUBLANE, tm + HALO, CONV_CH), F32)],
        compiler_params=pltpu.CompilerParams(dimension_semantics=("arbitrary",), vmem_limit_bytes=VMEM_LIMIT),
        name="inproj",
    )(x, tqc, tqs, tk, past, gmix, win, gq, wuq, gkv, wkv, cw, cb, lng, lnb, og)


def _kvexpand_kernel(ckv_ref, krp_ref, wkv_ref, k_ref, v_ref):
    kv = _dot(ckv_ref[...].astype(BF16), wkv_ref[...])
    krp = krp_ref[...]
    lane = lax.broadcasted_iota(jnp.int32, krp.shape, 1)
    one_at_sum = jnp.where(lane == V_DIM, 1.0, 0.0)
    for hd in range(N_HEADS):
        k_ref[:, hd * LANE:(hd + 1) * LANE] = (kv[:, hd * LANE:(hd + 1) * LANE] + krp).astype(BF16)
        v_ref[:, hd * LANE:(hd + 1) * LANE] = (
            kv[:, HEAD_W + hd * LANE:HEAD_W + (hd + 1) * LANE] + one_at_sum).astype(BF16)


def _kvexpand(ckv, krp, wkv, tm):
    n = ckv.shape[0]
    row = lambda w: pl.BlockSpec((tm, w), lambda i: (i, 0))
    return pl.pallas_call(
        _kvexpand_kernel,
        grid=(n // tm,),
        in_specs=[row(KV_LORA), row(LANE), _resident(wkv.shape)],
        out_specs=[row(HEAD_W), row(HEAD_W)],
        out_shape=[jax.ShapeDtypeStruct((n, HEAD_W), BF16), jax.ShapeDtypeStruct((n, HEAD_W), BF16)],
        compiler_params=pltpu.CompilerParams(dimension_semantics=("arbitrary",), vmem_limit_bytes=VMEM_LIMIT),
        name="kvexpand",
    )(ckv, krp, wkv)


def _scores(q, kblk):
    return lax.dot_general(q, kblk, (((1,), (1,)), ((), ())), preferred_element_type=F32)


def _softmax_pv(s, vblk, m, acc, mask=None):
    if mask is not None:
        s = jnp.where(mask, s, NEG)
    m_new = jnp.maximum(m, jnp.max(s, axis=-1, keepdims=True))
    p = jnp.exp2((s - m_new).astype(BF16))
    acc = jnp.exp2(m - m_new) * acc + _dot(p, vblk)
    return m_new, acc


def _attn_step(q, kblk, vblk, m, acc, mask=None):
    return _softmax_pv(_scores(q, kblk), vblk, m, acc, mask)


def _attn_finish(acc):
    return (acc / acc[:, V_DIM:V_DIM + 1])[:, :V_DIM]


def _attn_causal_kernel(q_ref, k_ref, v_ref, o_ref, a0_ref, a1_ref, b0_ref, b1_ref, p_ref, m_ref, acc_ref, *, tq, tk):
    qi = pl.program_id(2)
    lanes = [slice(j * LANE, (j + 1) * LANE) for j in range(2)]
    bufs = ((a0_ref, a1_ref), (b0_ref, b1_ref))
    halves = tq // tk

    def scores(g, into):
        for h, s_ref in enumerate(bufs[into]):
            start = pl.multiple_of(g * tq + h * tk, tk)
            for j, ls in enumerate(lanes):
                s_ref[j] = _scores(q_ref[0, :, ls], k_ref[0, pl.ds(start, tk), ls])

    def update(g, frm, rows=slice(0, tq), blocks=halves, masks=None):
        start = pl.multiple_of(g * tq, tq)
        for j, ls in enumerate(lanes):
            for r0 in range(rows.start, rows.stop, SOFTMAX_ROWS):
                rc = slice(r0, r0 + SOFTMAX_ROWS)
                s = [s_ref[j, rc, :] for s_ref in bufs[frm][:blocks]]
                if masks is not None:
                    local = slice(r0 - rows.start, r0 - rows.start + SOFTMAX_ROWS)
                    s = [x if mk is None else jnp.where(mk[local], x, NEG) for x, mk in zip(s, masks)]
                m_old = m_ref[j, rc, :]
                m_new = functools.reduce(jnp.maximum, [m_old] + [jnp.max(x, axis=-1, keepdims=True) for x in s])
                m_ref[j, rc, :] = m_new
                acc_ref[rc, ls] = acc_ref[rc, ls] * jnp.exp2(m_old - m_new)
                m_wide = jnp.concatenate([m_new] * (tk // LANE), axis=1)
                for h in range(blocks):
                    p_ref[j, rc, h * tk:(h + 1) * tk] = jnp.exp2((s[h] - m_wide).astype(BF16))
        keys = blocks * tk
        v2 = v_ref[0, pl.ds(start, keys), :]
        none = jnp.zeros((keys, LANE), BF16)
        acc_ref[rows, :] += (_dot(p_ref[0, rows, :keys], jnp.concatenate([v2[:, :LANE], none], axis=1))
                             + _dot(p_ref[1, rows, :keys], jnp.concatenate([none, v2[:, LANE:]], axis=1)))

    def step(g, frm):
        scores(g + 1, 1 - frm)
        update(g, frm)

    def two_steps(i, carry):
        step(2 * i, 0)
        step(2 * i + 1, 1)
        return carry

    def own_group(frm):
        chunk_of = lambda axis: lax.broadcasted_iota(jnp.int32, (tk, tk), axis) // CHUNK
        diag = chunk_of(1) <= chunk_of(0)
        for h in range(halves):
            update(qi, frm, slice(h * tk, (h + 1) * tk), h + 1, [None] * h + [diag])
        o_ref[0] = jnp.concatenate([_attn_finish(acc_ref[:, ls]) for ls in lanes], axis=1)

    m_ref[...] = jnp.full(m_ref.shape, NEG, F32)
    acc_ref[...] = jnp.zeros(acc_ref.shape, F32)
    scores(0, 0)
    lax.fori_loop(0, qi // 2, two_steps, 0)
    odd = qi % 2 == 1

    @pl.when(odd)
    def _():
        step(qi - 1, 0)
        own_group(1)

    @pl.when(jnp.logical_not(odd))
    def _():
        own_group(0)


def _attn_causal(q, k, v, tq, tk):
    assert tq == 2 * tk
    bsz, s, _ = q.shape
    kv_spec = pl.BlockSpec((1, s, 2 * LANE), lambda bi, hp, qi: (bi, 0, hp))
    return pl.pallas_call(
        functools.partial(_attn_causal_kernel, tq=tq, tk=tk),
        grid=(bsz, N_HEADS // 2, s // tq),
        in_specs=[pl.BlockSpec((1, tq, 2 * LANE), lambda bi, hp, qi: (bi, qi, hp)), kv_spec, kv_spec],
        out_specs=pl.BlockSpec((1, tq, 2 * V_DIM), lambda bi, hp, qi: (bi, qi, hp)),
        out_shape=jax.ShapeDtypeStruct((bsz, s, N_HEADS * V_DIM), F32),
        scratch_shapes=([pltpu.VMEM((2, tq, tk), F32)] * 4 + [pltpu.VMEM((2, tq, tq), BF16)]
                        + [pltpu.VMEM((2, tq, LANE), F32), pltpu.VMEM((tq, 2 * LANE), F32)]),
        compiler_params=pltpu.CompilerParams(dimension_semantics=("arbitrary",) * 3, vmem_limit_bytes=VMEM_LIMIT),
        name="attn_causal",
    )(q, k, v)


def _attn_open_kernel(q_ref, kc_ref, vc_ref, kn_ref, vn_ref, o_ref):
    tq = q_ref.shape[1]
    outs = []
    for j in range(2):
        ls = slice(j * LANE, (j + 1) * LANE)
        q = q_ref[0, :, ls]
        m = jnp.full((tq, 1), NEG, F32)
        acc = jnp.zeros((tq, LANE), F32)
        m, acc = _attn_step(q, kc_ref[0, :, ls], vc_ref[0, :, ls], m, acc)
        m, acc = _attn_step(q, kn_ref[0, :, ls], vn_ref[0, :, ls], m, acc)
        outs.append(_attn_finish(acc))
    o_ref[0] = jnp.concatenate(outs, axis=1)


def _attn_open(q, kc, vc, kn, vn):
    bsz, s, _ = q.shape
    past = kc.shape[1]
    new = pl.BlockSpec((1, s, 2 * LANE), lambda bi, hp: (bi, 0, hp))
    old = pl.BlockSpec((1, past, 2 * LANE), lambda bi, hp: (bi, 0, hp))
    return pl.pallas_call(
        _attn_open_kernel,
        grid=(bsz, N_HEADS // 2),
        in_specs=[new, old, old, new, new],
        out_specs=pl.BlockSpec((1, s, 2 * V_DIM), lambda bi, hp: (bi, 0, hp)),
        out_shape=jax.ShapeDtypeStruct((bsz, s, N_HEADS * V_DIM), F32),
        compiler_params=pltpu.CompilerParams(dimension_semantics=("arbitrary",) * 2, vmem_limit_bytes=VMEM_LIMIT),
        name="attn_open",
    )(q, kc, vc, kn, vn)


def _post_kernel(x_ref, attn_ref, conv_ref, p_ref, ga_ref, wout_ref, gffn_ref, wup_ref, wdn_ref, gple_ref,
                 wgate_ref, wproj_ref, gfin_ref, y_ref):
    aw = attn_ref.shape[1]
    a = _rms(attn_ref[...], ga_ref[...]).astype(BF16)
    x = x_ref[...] + _dot(a, wout_ref[:aw, :]) + _dot(conv_ref[...], wout_ref[aw:, :])
    h = _rms(x, gffn_ref[...]).astype(BF16)
    ff = None
    for c in range(wup_ref.shape[1] // FF_CHUNK):
        cs = slice(c * FF_CHUNK, (c + 1) * FF_CHUNK)
        f = jnp.maximum(_dot(h, wup_ref[:, cs]), 0.0)
        part = _dot((f * f).astype(BF16), wdn_ref[cs, :])
        ff = part if ff is None else ff + part
    x = x + ff
    gate = jax.nn.sigmoid(_dot(_rms(x, gple_ref[...]).astype(BF16), wgate_ref[...]))
    x = x + gate * _dot(p_ref[...].astype(BF16), wproj_ref[...])
    y_ref[...] = _rms(x, gfin_ref[...])


def _post(x, attn, conv, p, ga, wout, gffn, wup, wdn, gple, wgate, wproj, gfin, tm):
    n, d = x.shape
    row = lambda w: pl.BlockSpec((tm, w), lambda i: (i, 0))
    return pl.pallas_call(
        _post_kernel,
        grid=(n // tm,),
        in_specs=[row(d), row(attn.shape[1]), row(conv.shape[1]), row(p.shape[1]), _full(ga.shape),
                  _resident(wout.shape), _full(gffn.shape), _resident(wup.shape), _resident(wdn.shape),
                  _full(gple.shape), _resident(wgate.shape), _resident(wproj.shape), _full(gfin.shape)],
        out_specs=row(d),
        out_shape=jax.ShapeDtypeStruct((n, d), F32),
        compiler_params=pltpu.CompilerParams(dimension_semantics=("arbitrary",), vmem_limit_bytes=VMEM_LIMIT),
        name="post",
    )(x, attn, conv, p, ga, wout, gffn, wup, wdn, gple, wgate, wproj, gfin)


def _swap_halves(w):
    half = w.shape[-1] // 2
    return jnp.concatenate([-w[..., half:], w[..., :half]], axis=-1)


def _prep_weights(w_in, w_uq, w_ukv):
    o_kr = Q_LORA + KV_LORA
    w_kr = w_in[:, o_kr:o_kr + QK_ROPE]
    w_kr_sw = _swap_halves(w_kr)
    win = jnp.concatenate([w_in[:, :o_kr], w_in[:, o_kr + QK_ROPE:], w_kr, w_kr_sw, w_kr, w_kr_sw], axis=1)

    pad = LANE - QK_DIM
    w3 = w_uq.reshape(Q_LORA, N_HEADS, QK_DIM)
    rope = w3[..., QK_NOPE:]
    wa = jnp.concatenate([w3, jnp.zeros((Q_LORA, N_HEADS, pad), F32)], axis=-1)
    wb = jnp.concatenate([jnp.zeros((Q_LORA, N_HEADS, QK_NOPE), F32), _swap_halves(rope),
                          jnp.zeros((Q_LORA, N_HEADS, pad), F32)], axis=-1)
    wuq = jnp.concatenate([wa.reshape(Q_LORA, HEAD_W), wb.reshape(Q_LORA, HEAD_W)], axis=1)

    kv3 = w_ukv.reshape(KV_LORA, N_HEADS, QK_NOPE + V_DIM)
    wk = jnp.concatenate([kv3[..., :QK_NOPE], jnp.zeros((KV_LORA, N_HEADS, LANE - QK_NOPE), F32)], axis=-1)
    wv = jnp.concatenate([kv3[..., QK_NOPE:], jnp.zeros((KV_LORA, N_HEADS, LANE - V_DIM), F32)], axis=-1)
    wkv = jnp.concatenate([wk.reshape(KV_LORA, HEAD_W), wv.reshape(KV_LORA, HEAD_W)], axis=1)
    return win.astype(BF16), wuq.astype(BF16), wkv.astype(BF16)


def _rope_tables(pos):
    half = QK_ROPE // 2
    inv = ROPE_BASE ** (-jnp.arange(half, dtype=F32) / half)
    ang = pos.astype(F32)[:, None] * inv[None, :]
    cos, sin = jnp.cos(ang), jnp.sin(ang)
    cos2 = jnp.concatenate([cos, cos], axis=1)
    sin2 = jnp.concatenate([sin, sin], axis=1)
    n = pos.shape[0]
    scale = QK_DIM ** -0.5 * 1.4426950408889634
    tqc = scale * jnp.concatenate([jnp.ones((n, QK_NOPE), F32), cos2, jnp.zeros((n, LANE - QK_DIM), F32)], axis=1)
    tqs = scale * jnp.concatenate([jnp.zeros((n, QK_NOPE), F32), sin2, jnp.zeros((n, LANE - QK_DIM), F32)], axis=1)
    tk = jnp.concatenate([cos2, sin2, cos2, sin2], axis=1)
    return tqc, tqs, tk


def _row_tile(n, want):
    t = min(n, want)
    while n % t:
        t //= 2
    return t


def _layer(x, p, pos, past, lw, fin_g):
    bsz, s, d = x.shape
    n = bsz * s
    assert s >= HALO, "the new conv state is taken from this step's rows only"
    tqc, tqs, tk = _rope_tables(pos)
    if past is None:
        conv_past = jnp.zeros((bsz, HALO, CONV_CH), F32)
    else:
        conv_past = jnp.pad(past[2], ((0, 0), (HALO - (CONV_K - 1), 0), (0, 0)))
    q, k, v, ckv, kr, conv, tail = _inproj(
        x.reshape(n, d), tqc, tqs, tk, conv_past, lw["gmix"], lw["win"], lw["gq"], lw["wuq"], lw["gkv"], lw["wkv"],
        lw["conv_w"], lw["conv_b"], lw["ln_g"], lw["ln_b"], lw["conv_og"], _row_tile(s, ROW_TILE))
    q3, k3, v3 = (t.reshape(bsz, s, HEAD_W) for t in (q, k, v))
    if past is None:
        attn = _attn_causal(q3, k3, v3, _row_tile(s, ATTN_Q_ROWS), _row_tile(s, ATTN_Q_ROWS // 2))
    else:
        plen = past[0].shape[1]
        krp = jnp.pad(past[1].reshape(bsz * plen, QK_ROPE), ((0, 0), (QK_NOPE, LANE - QK_DIM)))
        kc, vc = _kvexpand(past[0].reshape(bsz * plen, KV_LORA), krp, lw["wkv"], _row_tile(bsz * plen, ROW_TILE))
        attn = _attn_open(q3, kc.reshape(bsz, plen, HEAD_W), vc.reshape(bsz, plen, HEAD_W), k3, v3)
    y = _post(x.reshape(n, d), attn.reshape(n, -1), conv, p.reshape(n, -1), lw["ga"], lw["wout"], lw["gffn"],
              lw["wup"], lw["wdn"], lw["gple"], lw["wgate"], lw["wproj"], fin_g, _row_tile(n, ROW_TILE))
    conv_new = tail[:, HALO - (CONV_K - 1):]
    return y.reshape(bsz, s, d), ckv.reshape(bsz, s, KV_LORA), kr.reshape(bsz, s, QK_ROPE), conv_new


def kernel(x_prompt, x_sample, cache_ckv, cache_krope, state_conv, p_prompt, p_sample, norm_mix_g, w_in, q_norm_g, w_uq, kv_norm_g, w_ukv, conv_w, conv_b, conv_ln_g, conv_ln_b, attn_out_g, conv_out_g, w_out, norm_ffn_g, w_ff_up, w_ff_down, norm_ple_g, w_ple_gate, w_ple_proj, norm_final_g):
    depth = w_in.shape[0]
    assert depth == 1, "the final RMSNorm is fused into the (single) layer's last kernel"
    past_len = cache_ckv.shape[2]
    pos_prompt = jnp.arange(x_prompt.shape[1])
    pos_sample = past_len + jnp.arange(x_sample.shape[1])
    row = lambda g: g.reshape(1, -1)
    outs_p, outs_s = [], []
    hp, hs = x_prompt, x_sample
    for i in range(depth):
        win, wuq, wkv = _prep_weights(w_in[i], w_uq[i], w_ukv[i])
        lw = dict(
            gmix=row(norm_mix_g[i]), win=win, gq=row(q_norm_g[i]), wuq=wuq, gkv=row(kv_norm_g[i]), wkv=wkv,
            conv_w=jnp.repeat(conv_w[i], SUBLANE, axis=0), conv_b=row(conv_b[i]),
            ln_g=row(conv_ln_g[i]), ln_b=row(conv_ln_b[i]), conv_og=row(conv_out_g[i]),
            ga=row(attn_out_g[i]), wout=w_out[i].astype(BF16), gffn=row(norm_ffn_g[i]),
            wup=w_ff_up[i].astype(BF16), wdn=w_ff_down[i].astype(BF16), gple=row(norm_ple_g[i]),
            wgate=w_ple_gate[i].astype(BF16), wproj=w_ple_proj[i].astype(BF16))
        fin = row(norm_final_g)
        hp, a, b, c = _layer(hp, p_prompt[i], pos_prompt, None, lw, fin)
        outs_p.append((a, b, c))
        hs, a, b, c = _layer(hs, p_sample[i], pos_sample, (cache_ckv[i], cache_krope[i], state_conv[i]), lw, fin)
        outs_s.append((a, b, c))
    stack = lambda outs, j: jnp.stack([o[j] for o in outs])
    return (hp, hs, stack(outs_p, 0), stack(outs_p, 1), stack(outs_p, 2),
            stack(outs_s, 0), stack(outs_s, 1), stack(outs_s, 2))
```
